```python
import math
import jax, jax.numpy as jnp
from jax import lax
import numpy as np

D_MODEL = 1024
BATCH = 32
SEQ = 2048
DEPTH = 2

HEAD_DIM = 64
MEM_LEN = 256
NORM_EPS = 1e-6
NEG = -1e30
SCALE = HEAD_DIM ** -0.5
A_HEADS = 4
A_CONFIGS = ((128, 1), (512, 4), (2048, 16))
B_HEADS = 4
SB_BLOCK = 128
C_HEADS = 4
C_KV_HEADS = 2
C_WINDOW = 128
D_HEADS = 4
MOBA_BLOCK = 256
MOBA_TOPK = 3
MOBA_QCHUNK = 16
BAND_BLOCK = 128
REL_BUCKETS = 32
REL_MAX_DIST = 2048
A_BIAS_LO = 0
C_BIAS_LO = A_HEADS
D_BIAS_LO = A_HEADS + C_HEADS
REL_HEADS = A_HEADS + C_HEADS + D_HEADS
X_HEADS = 4
X_HEAD_DIM = 64
X_W = X_HEADS * X_HEAD_DIM
X_SCALE = X_HEAD_DIM ** -0.5
D_FF = 4 * D_MODEL
A_W = A_HEADS * HEAD_DIM
B_W = B_HEADS * HEAD_DIM
C_QW = C_HEADS * HEAD_DIM
C_KVW = C_KV_HEADS * HEAD_DIM
D_W = D_HEADS * HEAD_DIM
IN_WIDTH = 3 * A_W + 3 * B_W + C_QW + 2 * C_KVW + 3 * D_W
MIX_WIDTH = A_W + B_W + C_QW + D_W

kernel_name = "hybrid_dilated_stickbreak_swa_moba_block"


def rmsnorm(x, g):
    xf = x.astype(jnp.float32)
    y = xf * lax.rsqrt(jnp.mean(xf * xf, axis=-1, keepdims=True) + NORM_EPS)
    return (y * g.astype(jnp.float32)).astype(x.dtype)


def rel_bucket(dist):
    max_exact = REL_BUCKETS // 2
    n = jnp.maximum(dist, 0)
    nf = jnp.maximum(n, 1).astype(jnp.float32)
    large = max_exact + (jnp.log(nf / max_exact) / math.log(REL_MAX_DIST / max_exact)
                         * (REL_BUCKETS - max_exact)).astype(jnp.int32)
    large = jnp.minimum(large, REL_BUCKETS - 1)
    return jnp.where(n < max_exact, n, large)


def rel_bias_heads(rel_table, dist, head_lo, n_heads):
    tab = rel_table[:, head_lo:head_lo + n_heads].astype(jnp.float32)
    return jnp.moveaxis(tab[rel_bucket(dist)], -1, 0)


def pad_to_multiple(x, axis, mult):
    n = x.shape[axis]
    pad = (-n) % mult
    if pad == 0:
        return x
    widths = [(0, 0)] * x.ndim
    widths[axis] = (0, pad)
    return jnp.pad(x, widths)


def to_blocks(x, blk):
    return x.reshape(x.shape[:-2] + (x.shape[-2] // blk, blk, x.shape[-1]))


def band_keys(xb):
    prev = jnp.concatenate([jnp.zeros_like(xb[..., :1, :, :]), xb[..., :-1, :, :]], axis=-3)
    return jnp.concatenate([prev, xb], axis=-2)


def band_geometry(n_blocks):
    qi = jnp.arange(BAND_BLOCK)[:, None]
    ki = jnp.arange(2 * BAND_BLOCK)[None, :]
    dist = qi + BAND_BLOCK - ki
    first_ok = (jnp.arange(n_blocks)[:, None, None] > 0) | (ki >= BAND_BLOCK)[None]
    return dist, first_ok


def dilated_attention(q, k, v, rel_table):
    B, H, S, E = q.shape
    blk = BAND_BLOCK
    nums, maxs, dens = [], [], []
    for window, dil in A_CONFIGS:
        span = window // dil
        L = S // dil

        def residue_blocks(t):
            t = t.reshape(B, H, L, dil, E).transpose(0, 1, 3, 2, 4)
            return to_blocks(pad_to_multiple(t, 3, blk), blk)

        qb = residue_blocks(q)
        kb = band_keys(residue_blocks(k))
        vb = band_keys(residue_blocks(v))
        nb = qb.shape[3]
        dist, first_ok = band_geometry(nb)
        valid = (dist >= 0) & (dist <= span) & first_ok
        bias = rel_bias_heads(rel_table, dist * dil, A_BIAS_LO, H)
        s = jnp.einsum("bhrnqe,bhrnke->bhrnqk", qb, kb) * SCALE + bias[None, :, None, None]
        s = jnp.where(valid, s, NEG)
        m = jnp.max(s, axis=-1)
        p = jnp.exp(s - m[..., None])
        l = jnp.sum(p, axis=-1)
        o = jnp.einsum("bhrnqk,bhrnke->bhrnqe", p, vb)

        def unres(t):
            t = t.reshape((B, H, dil, nb * blk) + t.shape[5:])[:, :, :, :L]
            t = jnp.swapaxes(t, 2, 3)
            return t.reshape((B, H, S) + t.shape[4:])

        nums.append(unres(o))
        maxs.append(unres(m))
        dens.append(unres(l))
    mx = jnp.stack(maxs)
    w = jnp.exp(mx - jnp.max(mx, axis=0, keepdims=True))
    num = jnp.einsum("cbhs,cbhse->bhse", w, jnp.stack(nums))
    den = jnp.sum(w * jnp.stack(dens), axis=0)
    return num / den[..., None]


def stick_breaking_attention(q, k, v):
    B, H, S, E = q.shape
    nblk = S // SB_BLOCK
    qs = q.reshape(B, H, nblk, SB_BLOCK, E).transpose(2, 0, 1, 3, 4)
    kpos = jnp.arange(S)

    def one_block(args):
        qb, i = args
        qpos = i * SB_BLOCK + jnp.arange(SB_BLOCK)
        z = jnp.einsum("bhqe,bhke->bhqk", qb, k) * SCALE
        past = kpos[None, :] < qpos[:, None]
        log_1m = jnp.where(past, jax.nn.log_sigmoid(-z), 0.0)
        suffix = lax.cumsum(log_1m, axis=3, reverse=True) - log_1m
        a = jnp.where(past, jnp.exp(jax.nn.log_sigmoid(z) + suffix), 0.0)
        return jnp.einsum("bhqk,bhke->bhqe", a, v)

    out = lax.map(one_block, (qs, jnp.arange(nblk)))
    return out.transpose(1, 2, 0, 3, 4).reshape(B, H, S, E)


def sliding_window_sink_attention(q, k, v, sinks, rel_table):
    B, Hq, S, E = q.shape
    G = k.shape[1]
    R = Hq // G
    blk = BAND_BLOCK
    qb = to_blocks(q.reshape(B, G, R, S, E), blk)
    kb = band_keys(to_blocks(k, blk))
    vb = band_keys(to_blocks(v, blk))
    nb = qb.shape[3]
    dist, first_ok = band_geometry(nb)
    valid = (dist >= 0) & (dist < C_WINDOW) & first_ok
    bias = rel_bias_heads(rel_table, dist, C_BIAS_LO, Hq).reshape(G, R, blk, 2 * blk)
    s = jnp.einsum("bgrnqe,bgnke->bgrnqk", qb, kb) * SCALE + bias[None, :, :, None]
    s = jnp.where(valid, s, NEG)
    sink = sinks.astype(jnp.float32).reshape(G, R)[None, :, :, None, None, None]
    m = jnp.maximum(jnp.max(s, axis=-1, keepdims=True), sink)
    p = jnp.exp(s - m)
    den = jnp.sum(p, axis=-1, keepdims=True) + jnp.exp(sink - m)
    o = jnp.einsum("bgrnqk,bgnke->bgrnqe", p / den, vb)
    return o.reshape(B, Hq, S, E)


def moba_attention(q, k, v, rel_table):
    B, H, S, E = q.shape
    blk = MOBA_BLOCK
    qp = pad_to_multiple(q, 2, blk)
    kp = pad_to_multiple(k, 2, blk)
    vp = pad_to_multiple(v, 2, blk)
    Sp = qp.shape[2]
    nblk = Sp // blk
    kblk = kp.reshape(B, H, nblk, blk, E)
    vblk = vp.reshape(B, H, nblk, blk, E)
    kmean = jnp.mean(kblk, axis=3)
    qblock = jnp.arange(Sp) // blk
    fully_past = jnp.arange(nblk)[None, :] < qblock[:, None]
    gate = jnp.where(fully_past, jnp.einsum("bhse,bhne->bhsn", qp, kmean), NEG)
    n_sel = min(MOBA_TOPK, nblk)
    _, sel = lax.top_k(gate, n_sel)
    sel_ok = sel < qblock[:, None]
    tab = rel_table[:, D_BIAS_LO:D_BIAS_LO + H].T.astype(jnp.float32)
    b_ix = jnp.arange(B)[:, None, None, None]
    h_ix = jnp.arange(H)[None, :, None, None]
    offs = jnp.arange(blk)

    def one_chunk(c):
        start = c * MOBA_QCHUNK
        qc = lax.dynamic_slice_in_dim(qp, start, MOBA_QCHUNK, axis=2)
        idx = lax.dynamic_slice_in_dim(sel, start, MOBA_QCHUNK, axis=2)
        ok = lax.dynamic_slice_in_dim(sel_ok, start, MOBA_QCHUNK, axis=2)
        tpos = start + jnp.arange(MOBA_QCHUNK)
        own = start // blk
        k_own = lax.dynamic_index_in_dim(kblk, own, axis=2, keepdims=False)
        v_own = lax.dynamic_index_in_dim(vblk, own, axis=2, keepdims=False)
        d_own = tpos[:, None] - (own * blk + offs)[None, :]
        s_own = jnp.einsum("bhqe,bhke->bhqk", qc, k_own) * SCALE + tab[:, rel_bucket(d_own)][None]
        s_own = jnp.where(d_own >= 0, s_own, NEG)
        k_sel = kblk[b_ix, h_ix, idx]
        v_sel = vblk[b_ix, h_ix, idx]
        d_sel = tpos[:, None, None] - (idx[..., None] * blk + offs)
        s_sel = (jnp.einsum("bhqe,bhqjke->bhqjk", qc, k_sel) * SCALE
                 + tab[h_ix[..., None], rel_bucket(d_sel)])
        s_sel = jnp.where(ok[..., None], s_sel, NEG).reshape(B, H, MOBA_QCHUNK, n_sel * blk)
        p = jax.nn.softmax(jnp.concatenate([s_own, s_sel], axis=-1), axis=-1)
        p_own = p[..., :blk]
        p_sel = p[..., blk:].reshape(B, H, MOBA_QCHUNK, n_sel, blk)
        return (jnp.einsum("bhqk,bhke->bhqe", p_own, v_own)
                + jnp.einsum("bhqjk,bhqjke->bhqe", p_sel, v_sel))

    out = lax.map(one_chunk, jnp.arange(Sp // MOBA_QCHUNK))
    return out.transpose(1, 2, 0, 3, 4).reshape(B, H, Sp, E)[:, :, :S]


def hybrid_mixer(h, w_in, g_group, sinks, w_out, rel_table):
    B, S, _ = h.shape
    proj = (h @ w_in).astype(jnp.float32)
    sizes = (A_W, A_W, A_W, B_W, B_W, B_W, C_QW, C_KVW, C_KVW, D_W, D_W, D_W)
    cuts = []
    acc = 0
    for sz in sizes[:-1]:
        acc += sz
        cuts.append(acc)
    aq, ak, av, bq, bk, bv, cq, ck, cv, dq, dk, dv = jnp.split(proj, cuts, axis=-1)

    def heads(t):
        return t.reshape(B, S, -1, HEAD_DIM).transpose(0, 2, 1, 3)

    ya = dilated_attention(heads(aq), heads(ak), heads(av), rel_table)
    yb = stick_breaking_attention(heads(bq), heads(bk), heads(bv))
    yc = sliding_window_sink_attention(heads(cq), heads(ck), heads(cv), sinks, rel_table)
    yd = moba_attention(heads(dq), heads(dk), heads(dv), rel_table)
    groups = []
    for y in (ya, yb, yc, yd):
        y = y.transpose(0, 2, 1, 3).reshape(B, S, -1)
        groups.append(y * lax.rsqrt(jnp.mean(y * y, axis=-1, keepdims=True) + NORM_EPS))
    y = jnp.concatenate(groups, axis=-1) * g_group.astype(jnp.float32)
    return y.astype(h.dtype) @ w_out


def memory_cross_attention(h, m, w_q, w_kv, w_o):
    B, S, _ = h.shape
    M = m.shape[1]
    q = (h @ w_q).astype(jnp.float32).reshape(B, S, X_HEADS, X_HEAD_DIM)
    kv = (m @ w_kv).astype(jnp.float32)
    k = kv[..., :X_W].reshape(B, M, X_HEADS, X_HEAD_DIM)
    v = kv[..., X_W:].reshape(B, M, X_HEADS, X_HEAD_DIM)
    p = jax.nn.softmax(jnp.einsum("bshe,bmhe->bhsm", q, k) * X_SCALE, axis=-1)
    o = jnp.einsum("bhsm,bmhe->bshe", p, v).reshape(B, S, X_W)
    return o.astype(h.dtype) @ w_o


def squared_relu_mlp(h, w_up, w_down):
    return jnp.square(jax.nn.relu(h @ w_up)) @ w_down


def setup_inputs(seed: int = 0) -> dict:
    key = jax.random.key(seed)
    ks = jax.random.split(key, 17)
    f32 = jnp.float32

    def nrm(k, shape, scale):
        return jax.random.normal(k, shape, f32) * scale

    def gain(k, shape):
        return 1.0 + 0.02 * jax.random.normal(k, shape, f32)

    return {
        "x": nrm(ks[0], (BATCH, SEQ, D_MODEL), 1.0),
        "mem": nrm(ks[1], (BATCH, MEM_LEN, D_MODEL), 1.0),
        "rel_table": nrm(ks[2], (REL_BUCKETS, REL_HEADS), 0.5),
        "g_mix": gain(ks[3], (DEPTH, D_MODEL)),
        "w_in": nrm(ks[4], (DEPTH, D_MODEL, IN_WIDTH), D_MODEL ** -0.5),
        "g_group": gain(ks[5], (DEPTH, MIX_WIDTH)),
        "sinks": nrm(ks[6], (DEPTH, C_HEADS), 0.5),
        "w_out": nrm(ks[7], (DEPTH, MIX_WIDTH, D_MODEL), MIX_WIDTH ** -0.5),
        "g_cross": gain(ks[8], (DEPTH, D_MODEL)),
        "g_mem": gain(ks[9], (DEPTH, D_MODEL)),
        "w_xq": nrm(ks[10], (DEPTH, D_MODEL, X_W), D_MODEL ** -0.5),
        "w_xkv": nrm(ks[11], (DEPTH, D_MODEL, 2 * X_W), D_MODEL ** -0.5),
        "w_xo": nrm(ks[12], (DEPTH, X_W, D_MODEL), X_W ** -0.5),
        "g_mlp": gain(ks[13], (DEPTH, D_MODEL)),
        "w_up": nrm(ks[14], (DEPTH, D_MODEL, D_FF), D_MODEL ** -0.5),
        "w_down": nrm(ks[15], (DEPTH, D_FF, D_MODEL), D_FF ** -0.5),
        "g_final": gain(ks[16], (D_MODEL,)),
    }


def reference(x, mem, rel_table, g_mix, w_in, g_group, sinks, w_out, g_cross, g_mem,
              w_xq, w_xkv, w_xo, g_mlp, w_up, w_down, g_final):
    for l in range(DEPTH):
        x = x + hybrid_mixer(rmsnorm(x, g_mix[l]), w_in[l], g_group[l], sinks[l], w_out[l], rel_table)
        x = x + memory_cross_attention(rmsnorm(x, g_cross[l]), rmsnorm(mem, g_mem[l]),
                                       w_xq[l], w_xkv[l], w_xo[l])
        x = x + squared_relu_mlp(rmsnorm(x, g_mlp[l]), w_up[l], w_down[l])
    return rmsnorm(x, g_final)
```

```python
import functools
import math

import numpy as np
import jax
import jax.numpy as jnp
from jax import lax
from jax.experimental import pallas as pl
from jax.experimental.pallas import tpu as pltpu

F32 = jnp.float32
BF16 = jnp.bfloat16

D_MODEL = 1024
HEAD_DIM = 64
PAIR = 2 * HEAD_DIM
NORM_EPS = 1e-6
NEG = -1e30
SCALE = HEAD_DIM ** -0.5
A_CONFIGS = ((128, 1), (512, 4), (2048, 16))
BAND = 128
C_WINDOW = 128
SB_BLOCK = 256
MOBA_BLOCK = 256
MOBA_TOPK = 3
REL_BUCKETS = 32
REL_MAX_DIST = 2048
A_BIAS_LO, C_BIAS_LO, D_BIAS_LO = 0, 4, 8
GROUP_W = 256
X_W = 256
D_FF = 4 * D_MODEL
COL_A, COL_B, COL_CQ, COL_CK, COL_CV, COL_D = 0, 768, 1536, 1792, 1920, 2048
IN_WIDTH = 2816
C_HEAD_ORDER = (0, 2, 1, 3)

VMEM_LIMIT = 48 * 1024 * 1024


def _cparams(*sem):
    return pltpu.CompilerParams(dimension_semantics=sem, vmem_limit_bytes=VMEM_LIMIT)


def _dot(a, b):
    return lax.dot_general(a, b, (((1,), (0,)), ((), ())), preferred_element_type=F32)


def _dot_nt(a, b):
    return lax.dot_general(a, b, (((1,), (1,)), ((), ())), preferred_element_type=F32)


def _head_mask(shape, hh):
    lane = lax.broadcasted_iota(jnp.int32, shape, len(shape) - 1)
    return (lane < HEAD_DIM) if hh == 0 else (lane >= HEAD_DIM)


def _bucket_np(dist):
    max_exact = REL_BUCKETS // 2
    n = np.maximum(dist, 0)
    nf = np.maximum(n, 1).astype(np.float64)
    large = max_exact + (np.log(nf / max_exact) / math.log(REL_MAX_DIST / max_exact)
                         * (REL_BUCKETS - max_exact)).astype(np.int32)
    large = np.minimum(large, REL_BUCKETS - 1)
    return np.where(n < max_exact, n, large).astype(np.int32)


def _static_buckets():
    qi = np.arange(BAND)[:, None]
    ki = np.arange(2 * BAND)[None, :]
    dist = qi + BAND - ki
    bkt_a = []
    for window, dil in A_CONFIGS:
        valid = (dist >= 0) & (dist <= window // dil)
        bkt_a.append(np.where(valid, _bucket_np(dist * dil), -1))
    valid_c = (dist >= 0) & (dist < C_WINDOW)
    bkt_c = np.where(valid_c, _bucket_np(dist), -1)[None]
    qd = np.arange(MOBA_BLOCK)[:, None]
    kd = np.arange(MOBA_BLOCK)[None, :]
    bkt_d = []
    for delta in range(8):
        dd = delta * MOBA_BLOCK + qd - kd
        bkt_d.append(np.where(dd >= 0, _bucket_np(dd), -1))
    return (np.stack(bkt_a).astype(np.int32), bkt_c.astype(np.int32),
            np.stack(bkt_d).astype(np.int32))


def _rel_bias_kernel(tab_ref, bkt_ref, o_ref, *, head_lo):
    h = head_lo + pl.program_id(1)
    b = bkt_ref[0]
    acc = jnp.full(b.shape, NEG, F32)
    for k in range(REL_BUCKETS):
        acc = jnp.where(b == k, tab_ref[h, k], acc)
    o_ref[0, 0] = acc


def _rel_bias(tab_t, buckets, head_lo):
    n, r, c = buckets.shape
    return pl.pallas_call(
        functools.partial(_rel_bias_kernel, head_lo=head_lo),
        grid=(n, 4),
        in_specs=[pl.BlockSpec(memory_space=pltpu.SMEM),
                  pl.BlockSpec((1, r, c), lambda i, h: (i, 0, 0))],
        out_specs=pl.BlockSpec((1, 1, r, c), lambda i, h: (i, h, 0, 0)),
        out_shape=jax.ShapeDtypeStruct((n, 4, r, c), F32),
        compiler_params=_cparams("arbitrary", "arbitrary"),
        name="rel_bias",
    )(tab_t, buckets)


def _rms(x, g):
    ms = jnp.mean(x * x, axis=-1, keepdims=True)
    return x * lax.rsqrt(ms + NORM_EPS) * g


def _norm_matmul_kernel(x_ref, g_ref, w_ref, o_ref):
    xn = _rms(x_ref[...], g_ref[...]).astype(BF16)
    o_ref[...] = _dot(xn, w_ref[...]).astype(o_ref.dtype)


def _norm_matmul(x, g, w, tm):
    t, d = x.shape
    n = w.shape[1]
    return pl.pallas_call(
        _norm_matmul_kernel,
        grid=(t // tm,),
        in_specs=[pl.BlockSpec((tm, d), lambda i: (i, 0)),
                  pl.BlockSpec((1, d), lambda i: (0, 0)),
                  pl.BlockSpec((d, n), lambda i: (0, 0))],
        out_specs=pl.BlockSpec((tm, n), lambda i: (i, 0)),
        out_shape=jax.ShapeDtypeStruct((t, n), BF16),
        compiler_params=_cparams("parallel"),
        name="norm_matmul",
    )(x, g.reshape(1, d), w)


def _mixer_a_kernel(q_ref, k_ref, v_ref, bias_ref, o_ref, qf, kf, vf, so0, so1, sm0, sm1, sl0, sl1):
    seq = q_ref.shape[1]
    pair = pl.program_id(1)
    chunk = 256
    stats = ((so0, sm0, sl0), (so1, sm1, sl1))

    def upcast(i, _):
        rows = pl.ds(pl.multiple_of(i * chunk, chunk), chunk)
        qf[rows, :] = q_ref[0, rows, :].astype(F32) * SCALE
        kf[rows, :] = k_ref[0, rows, :].astype(F32)
        vf[rows, :] = v_ref[0, rows, :].astype(F32)
        return 0

    lax.fori_loop(0, seq // chunk, upcast, 0)

    for hh in range(2):
        head = pair * 2 + hh
        hmask = _head_mask((BAND, PAIR), hh)
        so, sm, sl = stats[hh]
        for c, (_, dil) in enumerate(A_CONFIGS):
            n_blocks = seq // dil // BAND

            def rows(start, n, dil=dil):
                return pl.ds(start, n) if dil == 1 else pl.ds(start, n, stride=dil)

            def block(q_start, k_start, n_keys, bias, c=c, rows=rows, hmask=hmask, so=so, sm=sm, sl=sl):
                q_rows = rows(q_start, BAND)
                k_rows = rows(k_start, n_keys)
                qm = jnp.where(hmask, qf[q_rows, :], 0.0).astype(BF16)
                k = kf[k_rows, :].astype(BF16)
                v = vf[k_rows, :].astype(BF16)
                s = _dot_nt(qm, k) + bias
                m_cur = jnp.max(s, axis=-1, keepdims=True)
                if c == 0:
                    p = jnp.exp(s - m_cur)
                    sm[q_rows, :] = jnp.broadcast_to(m_cur, (BAND, PAIR))
                    sl[q_rows, :] = jnp.broadcast_to(jnp.sum(p, axis=-1, keepdims=True), (BAND, PAIR))
                    so[q_rows, :] = _dot(p.astype(BF16), v)
                else:
                    m_old = sm[q_rows, :]
                    m_new = jnp.maximum(m_old, m_cur)
                    alpha = jnp.exp(m_old - m_new)
                    p = jnp.exp(s - m_new[:, 0:1])
                    sm[q_rows, :] = m_new
                    sl[q_rows, :] = sl[q_rows, :] * alpha + jnp.sum(p, axis=-1, keepdims=True)
                    so[q_rows, :] = so[q_rows, :] * alpha + _dot(p.astype(BF16), v)

            def first(r, _, block=block, c=c, head=head):
                block(r, r, BAND, bias_ref[c, head, :, BAND:])
                return 0

            if dil == 1:
                first(0, 0)
            else:
                lax.fori_loop(0, dil, first, 0)

            if n_blocks > 1:
                shift = dil.bit_length() - 1

                def later(idx, _, block=block, c=c, head=head, dil=dil, shift=shift):
                    r = idx & (dil - 1)
                    n = 1 + (idx >> shift)
                    q_start = r + n * (dil * BAND)
                    k_start = q_start - dil * BAND
                    if dil == 1:
                        q_start = pl.multiple_of(q_start, BAND)
                        k_start = pl.multiple_of(k_start, BAND)
                    block(q_start, k_start, 2 * BAND, bias_ref[c, head])
                    return 0

                lax.fori_loop(0, dil * (n_blocks - 1), later, 0)

    def finish(i, _):
        rows = pl.ds(pl.multiple_of(i * chunk, chunk), chunk)
        o0 = so0[rows, :] / sl0[rows, :]
        o1 = so1[rows, :] / sl1[rows, :]
        o_ref[0, rows, :] = jnp.where(_head_mask((chunk, PAIR), 0), o0, o1).astype(o_ref.dtype)
        return 0

    lax.fori_loop(0, seq // chunk, finish, 0)


def _mixer_a(proj, bias_a):
    b, seq, _ = proj.shape
    blk = COL_A // PAIR

    def col(j):
        return pl.BlockSpec((1, seq, PAIR), lambda i, p: (i, 0, blk + 2 * j + p))

    return pl.pallas_call(
        _mixer_a_kernel,
        grid=(b, 2),
        in_specs=[col(0), col(1), col(2),
                  pl.BlockSpec(bias_a.shape, lambda i, p: (0, 0, 0, 0))],
        out_specs=pl.BlockSpec((1, seq, PAIR), lambda i, p: (i, 0, p)),
        out_shape=jax.ShapeDtypeStruct((b, seq, GROUP_W), BF16),
        scratch_shapes=[pltpu.VMEM((seq, PAIR), F32)] * 9,
        compiler_params=_cparams("parallel", "parallel"),
        name="mixer_a_dilated",
    )(proj, proj, proj, bias_a)


def _neg_softplus(z):
    return -(jnp.maximum(z, 0.0) + jnp.log(1.0 + jnp.exp(-jnp.abs(z))))


def _rev_cumsum(lm, tri):
    hi = lm.astype(BF16)
    lo = (lm - hi.astype(F32)).astype(BF16)
    return _dot(hi, tri) + _dot(lo, tri)


def _mixer_b_kernel(q_ref, k_ref, v_ref, tri_ref, o_ref):
    seq = q_ref.shape[1]
    t = SB_BLOCK
    row = lax.broadcasted_iota(jnp.int32, (t, t), 0)
    colv = lax.broadcasted_iota(jnp.int32, (t, t), 1)
    past = colv < row

    def q_block(i, _):
        q_rows = pl.ds(pl.multiple_of(i * t, t), t)
        q = q_ref[0, q_rows, :] * SCALE
        outs = []
        for hh in range(2):
            qm = jnp.where(_head_mask((t, PAIR), hh), q, 0.0).astype(BF16)
            z = _dot_nt(qm, k_ref[0, q_rows, :])
            lm = jnp.where(past, _neg_softplus(z), 0.0)
            cs = _rev_cumsum(lm, tri_ref[...])
            a = jnp.where(past, jnp.exp(z + cs), 0.0)
            acc = _dot(a.astype(BF16), v_ref[0, q_rows, :])
            carry = cs[:, 0:1]

            def k_block(step, state, qm=qm):
                acc, carry = state
                k_rows = pl.ds(pl.multiple_of((i - 1 - step) * t, t), t)
                z = _dot_nt(qm, k_ref[0, k_rows, :])
                cs = _rev_cumsum(_neg_softplus(z), tri_ref[...])
                a = jnp.exp(z + cs + carry)
                return acc + _dot(a.astype(BF16), v_ref[0, k_rows, :]), carry + cs[:, 0:1]

            acc, _ = lax.fori_loop(0, i, k_block, (acc, carry))
            outs.append(acc)
        o_ref[0, q_rows, :] = jnp.where(_head_mask((t, PAIR), 0), outs[0], outs[1]).astype(o_ref.dtype)
        return 0

    lax.fori_loop(0, seq // t, q_block, 0)


def _mixer_b(proj, tri):
    b, seq, _ = proj.shape
    blk = COL_B // PAIR

    def col(j):
        return pl.BlockSpec((1, seq, PAIR), lambda i, p: (i, 0, blk + 2 * j + p))

    return pl.pallas_call(
        _mixer_b_kernel,
        grid=(b, 2),
        in_specs=[col(0), col(1), col(2), pl.BlockSpec(tri.shape, lambda i, p: (0, 0))],
        out_specs=pl.BlockSpec((1, seq, PAIR), lambda i, p: (i, 0, p)),
        out_shape=jax.ShapeDtypeStruct((b, seq, GROUP_W), BF16),
        compiler_params=_cparams("parallel", "parallel"),
        name="mixer_b_stickbreak",
    )(proj, proj, proj, tri)


def _mixer_c_kernel(sink_ref, q_ref, k_ref, v_ref, bias_ref, o_ref):
    seq = q_ref.shape[1]

    for pb in range(2):
        lanes = slice(pb * PAIR, (pb + 1) * PAIR)

        def block(q_start, k_start, n_keys, bias_cols, pb=pb, lanes=lanes):
            q_rows = pl.ds(q_start, BAND)
            k_rows = pl.ds(k_start, n_keys)
            q = q_ref[0, q_rows, lanes] * SCALE
            k = k_ref[0, k_rows, :]
            v = v_ref[0, k_rows, :]
            outs = []
            for hh in range(2):
                head = hh * 2 + pb
                sink = sink_ref[head]
                qm = jnp.where(_head_mask((BAND, PAIR), hh), q, 0.0).astype(BF16)
                s = _dot_nt(qm, k) + bias_ref[0, head, :, bias_cols]
                m = jnp.maximum(jnp.max(s, axis=-1, keepdims=True), sink)
                p = jnp.exp(s - m)
                den = jnp.sum(p, axis=-1, keepdims=True) + jnp.exp(sink - m)
                outs.append(_dot(p.astype(BF16), v) / den)
            o_ref[0, q_rows, lanes] = jnp.where(_head_mask((BAND, PAIR), 0), outs[0], outs[1]).astype(o_ref.dtype)

        block(0, 0, BAND, slice(BAND, 2 * BAND))

        def later(n, _, block=block):
            q_start = pl.multiple_of(n * BAND, BAND)
            block(q_start, pl.multiple_of(q_start - BAND, BAND), 2 * BAND, slice(0, 2 * BAND))
            return 0

        lax.fori_loop(1, seq // BAND, later, 0)


def _mixer_c(proj, bias_c, sinks):
    b, seq, _ = proj.shape
    return pl.pallas_call(
        _mixer_c_kernel,
        grid=(b,),
        in_specs=[pl.BlockSpec(memory_space=pltpu.SMEM),
                  pl.BlockSpec((1, seq, GROUP_W), lambda i: (i, 0, COL_CQ // GROUP_W)),
                  pl.BlockSpec((1, seq, PAIR), lambda i: (i, 0, COL_CK // PAIR)),
                  pl.BlockSpec((1, seq, PAIR), lambda i: (i, 0, COL_CV // PAIR)),
                  pl.BlockSpec(bias_c.shape, lambda i: (0, 0, 0, 0))],
        out_specs=pl.BlockSpec((1, seq, GROUP_W), lambda i: (i, 0, 0)),
        out_shape=jax.ShapeDtypeStruct((b, seq, GROUP_W), BF16),
        compiler_params=_cparams("parallel"),
        name="mixer_c_swa_sink",
    )(sinks, proj, proj, proj, bias_c)


def _mixer_d_kernel(q_ref, k_ref, v_ref, bias_ref, o_ref, sel_scr):
    seq = q_ref.shape[1]
    t = MOBA_BLOCK
    n_blk = seq // t

    kmean = jnp.concatenate(
        [jnp.sum(k_ref[0, j * t:(j + 1) * t, :].astype(F32), axis=0, keepdims=True) for j in range(n_blk)],
        axis=0) * (1.0 / t)
    km_hi = kmean.astype(BF16)
    km_lo = (kmean - km_hi.astype(F32)).astype(BF16)
    blk_id = lax.broadcasted_iota(jnp.int32, (n_blk, t), 0)

    def q_block(i, _):
        q_rows = pl.ds(pl.multiple_of(i * t, t), t)
        q = q_ref[0, q_rows, :]
        outs = []
        for hh in range(2):
            qm = jnp.where(_head_mask((t, PAIR), hh), q, 0.0).astype(BF16)
            gate = _dot_nt(km_hi, qm) + _dot_nt(km_lo, qm)
            gate = jnp.where(blk_id < i, gate, NEG)
            rank = jnp.zeros((n_blk, t), jnp.int32)
            for mth in range(n_blk):
                gm = gate[mth:mth + 1, :]
                ahead = (gm > gate) | ((gm == gate) & (mth < blk_id))
                rank = rank + ahead.astype(jnp.int32)
            chosen = (rank < MOBA_TOPK) & (blk_id < i)
            sel = jnp.where(chosen, 0.0, NEG).T
            for j in range(n_blk):
                sel_scr[j] = sel[:, j:j + 1]

            qs = qm * SCALE
            s = _dot_nt(qs, k_ref[0, q_rows, :]) + bias_ref[0, hh]
            m = jnp.max(s, axis=-1, keepdims=True)
            p = jnp.exp(s - m)
            l = jnp.sum(p, axis=-1, keepdims=True)
            acc = _dot(p.astype(BF16), v_ref[0, q_rows, :])

            def k_block(j, state, qs=qs, hh=hh):
                m, l, acc = state
                k_rows = pl.ds(pl.multiple_of(j * t, t), t)
                s = _dot_nt(qs, k_ref[0, k_rows, :]) + bias_ref[i - j, hh] + sel_scr[j]
                m_new = jnp.maximum(m, jnp.max(s, axis=-1, keepdims=True))
                alpha = jnp.exp(m - m_new)
                p = jnp.exp(s - m_new)
                l = l * alpha + jnp.sum(p, axis=-1, keepdims=True)
                acc = acc * alpha + _dot(p.astype(BF16), v_ref[0, k_rows, :])
                return m_new, l, acc

            m, l, acc = lax.fori_loop(0, i, k_block, (m, l, acc))
            outs.append(acc / l)
        o_ref[0, q_rows, :] = jnp.where(_head_mask((t, PAIR), 0), outs[0], outs[1]).astype(o_ref.dtype)
        return 0

    lax.fori_loop(0, n_blk, q_block, 0)


def _mixer_d(proj, bias_d):
    b, seq, _ = proj.shape
    blk = COL_D // PAIR
    n_blk = seq // MOBA_BLOCK

    def col(j):
        return pl.BlockSpec((1, seq, PAIR), lambda p, i: (i, 0, blk + 2 * j + p))

    return pl.pallas_call(
        _mixer_d_kernel,
        grid=(2, b),
        in_specs=[col(0), col(1), col(2),
                  pl.BlockSpec((n_blk, 2, MOBA_BLOCK, MOBA_BLOCK), lambda p, i: (0, p, 0, 0))],
        out_specs=pl.BlockSpec((1, seq, PAIR), lambda p, i: (i, 0, p)),
        out_shape=jax.ShapeDtypeStruct((b, seq, GROUP_W), BF16),
        scratch_shapes=[pltpu.VMEM((n_blk, MOBA_BLOCK, 1), F32)],
        compiler_params=_cparams("parallel", "parallel"),
        name="mixer_d_moba",
    )(proj, proj, proj, bias_d)


def _mix_out_kernel(x_ref, ya_ref, yb_ref, yc_ref, yd_ref, g_ref, w_ref, o_ref):
    acc = x_ref[...]
    for gi, y_ref in enumerate((ya_ref, yb_ref, yc_ref, yd_ref)):
        cols = slice(gi * GROUP_W, (gi + 1) * GROUP_W)
        yn = _rms(y_ref[...].astype(F32), g_ref[:, cols]).astype(BF16)
        acc = acc + _dot(yn, w_ref[cols, :])
    o_ref[...] = acc


def _mix_out(x, ys, g, w, tm):
    t, d = x.shape
    y_spec = pl.BlockSpec((tm, GROUP_W), lambda i: (i, 0))
    return pl.pallas_call(
        _mix_out_kernel,
        grid=(t // tm,),
        in_specs=[pl.BlockSpec((tm, d), lambda i: (i, 0)), y_spec, y_spec, y_spec, y_spec,
                  pl.BlockSpec((1, 4 * GROUP_W), lambda i: (0, 0)),
                  pl.BlockSpec((4 * GROUP_W, d), lambda i: (0, 0))],
        out_specs=pl.BlockSpec((tm, d), lambda i: (i, 0)),
        out_shape=jax.ShapeDtypeStruct((t, d), F32),
        compiler_params=_cparams("parallel"),
        name="mix_out",
    )(x, *ys, g.reshape(1, -1), w)


def _cross_kernel(x_ref, g_ref, wq_ref, kv_ref, wo_ref, o_ref):
    x = x_ref[0]
    h = _rms(x, g_ref[...]).astype(BF16)
    q = (_dot(h, wq_ref[...]) * SCALE).astype(BF16)
    tm = x.shape[0]
    acc = x
    for pair in range(2):
        lanes = slice(pair * PAIR, (pair + 1) * PAIR)
        k = kv_ref[0, :, lanes]
        v = kv_ref[0, :, X_W + pair * PAIR:X_W + (pair + 1) * PAIR]
        outs = []
        for hh in range(2):
            qm = jnp.where(_head_mask((tm, PAIR), hh), q[:, lanes], 0.0).astype(BF16)
            s = _dot_nt(qm, k)
            m = jnp.max(s, axis=-1, keepdims=True)
            p = jnp.exp(s - m)
            l = jnp.sum(p, axis=-1, keepdims=True)
            outs.append(_dot(p.astype(BF16), v) / l)
        o = jnp.where(_head_mask((tm, PAIR), 0), outs[0], outs[1]).astype(BF16)
        acc = acc + _dot(o, wo_ref[lanes, :])
    o_ref[0] = acc


def _cross(x, g, wq, kv, wo, tm):
    b, seq, d = x.shape
    mem_len = kv.shape[1]
    return pl.pallas_call(
        _cross_kernel,
        grid=(b, seq // tm),
        in_specs=[pl.BlockSpec((1, tm, d), lambda i, j: (i, j, 0)),
                  pl.BlockSpec((1, d), lambda i, j: (0, 0)),
                  pl.BlockSpec((d, X_W), lambda i, j: (0, 0)),
                  pl.BlockSpec((1, mem_len, 2 * X_W), lambda i, j: (i, 0, 0)),
                  pl.BlockSpec((X_W, d), lambda i, j: (0, 0))],
        out_specs=pl.BlockSpec((1, tm, d), lambda i, j: (i, j, 0)),
        out_shape=jax.ShapeDtypeStruct((b, seq, d), F32),
        compiler_params=_cparams("parallel", "parallel"),
        name="cross_attention",
    )(x, g.reshape(1, d), wq, kv, wo)


def _mlp_kernel(x_ref, g_ref, wu_ref, wd_ref, gf_ref, o_ref, xn_scr, acc_scr, *, final_norm):
    j = pl.program_id(1)

    @pl.when(j == 0)
    def _():
        x = x_ref[...]
        xn_scr[...] = _rms(x, g_ref[...]).astype(BF16)
        acc_scr[...] = x

    h = _dot(xn_scr[...], wu_ref[...])
    h = jnp.square(jnp.maximum(h, 0.0)).astype(BF16)
    acc_scr[...] += _dot(h, wd_ref[...])

    @pl.when(j == pl.num_programs(1) - 1)
    def _():
        y = acc_scr[...]
        if final_norm:
            y = _rms(y, gf_ref[...])
        o_ref[...] = y


def _mlp(x, g, wu, wd, g_final, final_norm, tm, tf):
    t, d = x.shape
    ff = wu.shape[1]
    return pl.pallas_call(
        functools.partial(_mlp_kernel, final_norm=final_norm),
        grid=(t // tm, ff // tf),
        in_specs=[pl.BlockSpec((tm, d), lambda i, j: (i, 0)),
                  pl.BlockSpec((1, d), lambda i, j: (0, 0)),
                  pl.BlockSpec((d, tf), lambda i, j: (0, j)),
                  pl.BlockSpec((tf, d), lambda i, j: (j, 0)),
                  pl.BlockSpec((1, d), lambda i, j: (0, 0))],
        out_specs=pl.BlockSpec((tm, d), lambda i, j: (i, 0)),
        out_shape=jax.ShapeDtypeStruct((t, d), F32),
        scratch_shapes=[pltpu.VMEM((tm, d), BF16), pltpu.VMEM((tm, d), F32)],
        compiler_params=_cparams("parallel", "arbitrary"),
        name="mlp",
    )(x, g.reshape(1, d), wu, wd, g_final.reshape(1, d))


def _permute_c_heads(a, axis, lo):
    idx = np.arange(a.shape[axis])
    seg = np.concatenate([lo + h * HEAD_DIM + np.arange(HEAD_DIM) for h in C_HEAD_ORDER])
    idx[lo:lo + GROUP_W] = seg
    return jnp.take(a, jnp.asarray(idx), axis=axis)


def kernel(x, mem, rel_table, g_mix, w_in, g_group, sinks, w_out, g_cross, g_mem, w_xq, w_xkv, w_xo,
           g_mlp, w_up, w_down, g_final):
    b, seq, d = x.shape
    mem_len = mem.shape[1]
    depth = w_in.shape[0]
    tokens = b * seq

    bkt_a, bkt_c, bkt_d = _static_buckets()
    tab_t = rel_table.astype(F32).T
    bias_a = _rel_bias(tab_t, jnp.asarray(bkt_a), A_BIAS_LO)
    bias_c = _rel_bias(tab_t, jnp.asarray(bkt_c), C_BIAS_LO)
    bias_d = _rel_bias(tab_t, jnp.asarray(bkt_d), D_BIAS_LO)
    tri = jnp.asarray(np.tril(np.ones((SB_BLOCK, SB_BLOCK), np.float32)), dtype=BF16)

    c_out_lo = 2 * GROUP_W
    xf = x.reshape(tokens, d)
    mem_f = mem.reshape(b * mem_len, d)
    for l in range(depth):
        w_in_l = _permute_c_heads(w_in[l], 1, COL_CQ).astype(BF16)
        g_group_l = _permute_c_heads(g_group[l], 0, c_out_lo)
        w_out_l = _permute_c_heads(w_out[l], 0, c_out_lo).astype(BF16)

        proj = _norm_matmul(xf, g_mix[l], w_in_l, tm=512).reshape(b, seq, IN_WIDTH)
        ya = _mixer_a(proj, bias_a)
        yb = _mixer_b(proj, tri)
        yc = _mixer_c(proj, bias_c, sinks[l].astype(F32))
        yd = _mixer_d(proj, bias_d)
        ys = [y.reshape(tokens, GROUP_W) for y in (ya, yb, yc, yd)]
        xf = _mix_out(xf, ys, g_group_l, w_out_l, tm=512)

        kv = _norm_matmul(mem_f, g_mem[l], w_xkv[l].astype(BF16), tm=512).reshape(b, mem_len, 2 * X_W)
        xf = _cross(xf.reshape(b, seq, d), g_cross[l], w_xq[l].astype(BF16), kv,
                    w_xo[l].astype(BF16), tm=512).reshape(tokens, d)

        xf = _mlp(xf, g_mlp[l], w_up[l].astype(BF16), w_down[l].astype(BF16), g_final,
                  final_norm=(l == depth - 1), tm=1024, tf=512)
    return xf.reshape(b, seq, d)
```

```python
import functools
import math

import numpy as np
import jax
import jax.numpy as jnp
from jax import lax
from jax.experimental import pallas as pl
from jax.experimental.pallas import tpu as pltpu

F32 = jnp.float32
BF16 = jnp.bfloat16

D_MODEL = 1024
HEAD_DIM = 64
PAIR = 2 * HEAD_DIM
NORM_EPS = 1e-6
NEG = -1e30
SCALE = HEAD_DIM ** -0.5
A_CONFIGS = ((128, 1), (512, 4), (2048, 16))
BAND = 128
C_WINDOW = 128
SB_BLOCK = 256
SB_SKIP = -100.0
MOBA_BLOCK = 256
MOBA_TOPK = 3
REL_BUCKETS = 32
REL_MAX_DIST = 2048
A_BIAS_LO, C_BIAS_LO, D_BIAS_LO = 0, 4, 8
GROUP_W = 256
X_W = 256
D_FF = 4 * D_MODEL
COL_A, COL_B, COL_CQ, COL_CK, COL_CV, COL_D = 0, 768, 1536, 1792, 1920, 2048
IN_WIDTH = 2816
C_HEAD_ORDER = (0, 2, 1, 3)

VMEM_LIMIT = 48 * 1024 * 1024


def _cparams(*sem):
    return pltpu.CompilerParams(dimension_semantics=sem, vmem_limit_bytes=VMEM_LIMIT)


def _dot(a, b):
    return lax.dot_general(a, b, (((1,), (0,)), ((), ())), preferred_element_type=F32)


def _dot_nt(a, b):
    return lax.dot_general(a, b, (((1,), (1,)), ((), ())), preferred_element_type=F32)


def _head_mask(shape, hh):
    lane = lax.broadcasted_iota(jnp.int32, shape, len(shape) - 1)
    return (lane < HEAD_DIM) if hh == 0 else (lane >= HEAD_DIM)


def _rows(start, size, align):
    if isinstance(start, int):
        return pl.ds(start, size)
    return pl.ds(pl.multiple_of(start, align), size)


def _bucket_np(dist):
    max_exact = REL_BUCKETS // 2
    n = np.maximum(dist, 0)
    nf = np.maximum(n, 1).astype(np.float64)
    large = max_exact + (np.log(nf / max_exact) / math.log(REL_MAX_DIST / max_exact)
                         * (REL_BUCKETS - max_exact)).astype(np.int32)
    large = np.minimum(large, REL_BUCKETS - 1)
    return np.where(n < max_exact, n, large).astype(np.int32)


def _static_buckets():
    qi = np.arange(BAND)[:, None]
    ki = np.arange(2 * BAND)[None, :]
    dist = qi + BAND - ki
    bkt_a = []
    for window, dil in A_CONFIGS:
        valid = (dist >= 0) & (dist <= window // dil)
        bkt_a.append(np.where(valid, _bucket_np(dist * dil), -1))
    valid_c = (dist >= 0) & (dist < C_WINDOW)
    bkt_c = np.where(valid_c, _bucket_np(dist), -1)[None]
    qd = np.arange(MOBA_BLOCK)[:, None]
    kd = np.arange(MOBA_BLOCK)[None, :]
    bkt_d = []
    for delta in range(8):
        dd = delta * MOBA_BLOCK + qd - kd
        bkt_d.append(np.where(dd >= 0, _bucket_np(dd), -1))
    return (np.stack(bkt_a).astype(np.int32), bkt_c.astype(np.int32),
            np.stack(bkt_d).astype(np.int32))


def _rel_bias_kernel(tab_ref, bkt_ref, o_ref, *, head_lo):
    h = head_lo + pl.program_id(1)
    b = bkt_ref[0]
    acc = jnp.full(b.shape, NEG, F32)
    for k in range(REL_BUCKETS):
        acc = jnp.where(b == k, tab_ref[h, k], acc)
    o_ref[0, 0] = acc


def _rel_bias(tab_t, buckets, head_lo):
    n, r, c = buckets.shape
    return pl.pallas_call(
        functools.partial(_rel_bias_kernel, head_lo=head_lo),
        grid=(n, 4),
        in_specs=[pl.BlockSpec(memory_space=pltpu.SMEM),
                  pl.BlockSpec((1, r, c), lambda i, h: (i, 0, 0))],
        out_specs=pl.BlockSpec((1, 1, r, c), lambda i, h: (i, h, 0, 0)),
        out_shape=jax.ShapeDtypeStruct((n, 4, r, c), F32),
        compiler_params=_cparams("arbitrary", "arbitrary"),
        name="rel_bias",
    )(tab_t, buckets)


def _rms(x, g):
    ms = jnp.mean(x * x, axis=-1, keepdims=True)
    return x * lax.rsqrt(ms + NORM_EPS) * g


def _norm_matmul_kernel(x_ref, g_ref, w_ref, o_ref):
    xn = _rms(x_ref[...], g_ref[...]).astype(BF16)
    o_ref[...] = _dot(xn, w_ref[...]).astype(o_ref.dtype)


def _norm_matmul(x, g, w, tm):
    t, d = x.shape
    n = w.shape[1]
    return pl.pallas_call(
        _norm_matmul_kernel,
        grid=(t // tm,),
        in_specs=[pl.BlockSpec((tm, d), lambda i: (i, 0)),
                  pl.BlockSpec((1, d), lambda i: (0, 0)),
                  pl.BlockSpec((d, n), lambda i: (0, 0))],
        out_specs=pl.BlockSpec((tm, n), lambda i: (i, 0)),
        out_shape=jax.ShapeDtypeStruct((t, n), BF16),
        compiler_params=_cparams("parallel"),
        name="norm_matmul",
    )(x, g.reshape(1, d), w)


A_TILES_PER_STEP = {1: 5, 4: 4, 16: 4}


def _mixer_a_kernel(q_ref, k_ref, v_ref, bias_ref, o_ref, qf, kf, vf, *stats):
    seq = q_ref.shape[1]
    pair = pl.program_id(1)
    chunk = 256
    hm0 = _head_mask((BAND, PAIR), 0)

    def upcast(i, _):
        rows = _rows(i * chunk, chunk, chunk)
        qf[rows, :] = q_ref[0, rows, :].astype(F32) * SCALE
        kf[rows, :] = k_ref[0, rows, :].astype(F32)
        vf[rows, :] = v_ref[0, rows, :].astype(F32)
        return 0

    lax.fori_loop(0, seq // chunk, upcast, 0)

    for c, (_, dil) in enumerate(A_CONFIGS):
        so, sm, sl = stats[3 * c:3 * c + 3]
        n_blocks = seq // dil // BAND
        per_step = A_TILES_PER_STEP[dil]

        def rows(start, n, dil=dil):
            if dil == 1:
                return _rows(start, n, BAND)
            return pl.ds(start, n, stride=dil)

        def tile(q_start, k_start, n_keys, bias_cols, c=c, rows=rows, so=so, sm=sm, sl=sl):
            q_rows = rows(q_start, BAND)
            k_rows = rows(k_start, n_keys)
            q = qf[q_rows, :]
            k = kf[k_rows, :].astype(BF16)
            v = vf[k_rows, :].astype(BF16)
            res = []
            for hh in range(2):
                qm = jnp.where(_head_mask((BAND, PAIR), hh), q, 0.0).astype(BF16)
                s = _dot_nt(qm, k) + bias_ref[c, pair * 2 + hh, :, bias_cols]
                m = jnp.max(s, axis=-1, keepdims=True)
                p = jnp.exp(s - m)
                res.append((m, jnp.sum(p, axis=-1, keepdims=True), _dot(p.astype(BF16), v)))
            (m0, l0, o0), (m1, l1, o1) = res
            sm[q_rows, :] = jnp.where(hm0, m0, m1)
            sl[q_rows, :] = jnp.where(hm0, l0, l1)
            so[q_rows, :] = jnp.where(hm0, o0, o1)

        first_cols = slice(BAND, 2 * BAND)
        all_cols = slice(0, 2 * BAND)
        span = dil * BAND

        if dil == 1:
            tile(0, 0, BAND, first_cols)
        elif dil <= per_step:
            for r in range(dil):
                tile(r, r, BAND, first_cols)
        else:
            def first_step(g, _, tile=tile, per_step=per_step):
                for u in range(per_step):
                    r = g * per_step + u
                    tile(r, r, BAND, first_cols)
                return 0

            lax.fori_loop(0, dil // per_step, first_step, 0)

        if n_blocks > 1:
            if dil == 1:
                def later_step(g, _, tile=tile, per_step=per_step, span=span):
                    for u in range(per_step):
                        q_start = (1 + g * per_step + u) * span
                        tile(q_start, q_start - span, 2 * BAND, all_cols)
                    return 0

                lax.fori_loop(0, (n_blocks - 1) // per_step, later_step, 0)
            else:
                def later_step(n, _, tile=tile, dil=dil, span=span):
                    for r in range(dil):
                        q_start = r + n * span
                        tile(q_start, q_start - span, 2 * BAND, all_cols)
                    return 0

                lax.fori_loop(1, n_blocks, later_step, 0)

    def merge(i, _):
        rows = _rows(i * BAND, BAND, BAND)
        ms = [stats[3 * c + 1][rows, :] for c in range(3)]
        mx = jnp.maximum(jnp.maximum(ms[0], ms[1]), ms[2])
        ws = [jnp.exp(m - mx) for m in ms]
        num = sum(w * stats[3 * c][rows, :] for c, w in enumerate(ws))
        den = sum(w * stats[3 * c + 2][rows, :] for c, w in enumerate(ws))
        o_ref[0, rows, :] = (num / den).astype(o_ref.dtype)
        return 0

    lax.fori_loop(0, seq // BAND, merge, 0)


def _mixer_a(proj, bias_a):
    b, seq, _ = proj.shape
    blk = COL_A // PAIR

    def col(j):
        return pl.BlockSpec((1, seq, PAIR), lambda i, p: (i, 0, blk + 2 * j + p))

    return pl.pallas_call(
        _mixer_a_kernel,
        grid=(b, 2),
        in_specs=[col(0), col(1), col(2),
                  pl.BlockSpec(bias_a.shape, lambda i, p: (0, 0, 0, 0))],
        out_specs=pl.BlockSpec((1, seq, PAIR), lambda i, p: (i, 0, p)),
        out_shape=jax.ShapeDtypeStruct((b, seq, GROUP_W), BF16),
        scratch_shapes=[pltpu.VMEM((seq, PAIR), F32)] * (3 + 3 * len(A_CONFIGS)),
        compiler_params=_cparams("parallel", "parallel"),
        name="mixer_a_dilated",
    )(proj, proj, proj, bias_a)


def _neg_softplus(z):
    return -(jnp.maximum(z, 0.0) + jnp.log(1.0 + jnp.exp(-jnp.abs(z))))


def _rev_cumsum(lm, tri):
    hi = lm.astype(BF16)
    lo = (lm - hi.astype(F32)).astype(BF16)
    return _dot(hi, tri) + _dot(lo, tri)


def _mixer_b_kernel(q_ref, k_ref, v_ref, tri_ref, o_ref):
    seq = q_ref.shape[1]
    t = SB_BLOCK
    n_heads = 4
    row = lax.broadcasted_iota(jnp.int32, (t, t), 0)
    colv = lax.broadcasted_iota(jnp.int32, (t, t), 1)
    past = colv < row
    masks = [_head_mask((t, PAIR), hh) for hh in range(2)]

    def lanes(h):
        return slice((h // 2) * PAIR, (h // 2 + 1) * PAIR)

    k_max = []
    for h in range(n_heads):
        def norm_step(i, best, h=h):
            kk = k_ref[0, _rows(i * t, t, t), lanes(h)].astype(F32)
            sq = jnp.sum(jnp.where(masks[h % 2], kk * kk, 0.0), axis=-1, keepdims=True)
            return jnp.maximum(best, sq)

        sq_max = lax.fori_loop(0, seq // t, norm_step, jnp.zeros((t, 1), F32))
        k_max.append(jnp.sqrt(jnp.max(sq_max, axis=0, keepdims=True)))

    def q_block(i, _):
        q_rows = _rows(i * t, t, t)
        qms, accs, carries, bounds = [], [], [], []
        for h in range(n_heads):
            q = q_ref[0, q_rows, lanes(h)] * SCALE
            qm = jnp.where(masks[h % 2], q, 0.0).astype(BF16)
            qf = qm.astype(F32)
            z = _dot_nt(qm, k_ref[0, q_rows, lanes(h)])
            lm = jnp.where(past, _neg_softplus(z), 0.0)
            cs = _rev_cumsum(lm, tri_ref[...])
            a = jnp.where(past, jnp.exp(z + cs), 0.0)
            qms.append(qm)
            accs.append(_dot(a.astype(BF16), v_ref[0, q_rows, lanes(h)]))
            carries.append(cs[:, 0:1])
            bounds.append(jnp.sqrt(jnp.sum(qf * qf, axis=-1, keepdims=True)) * k_max[h])

        def live(carries):
            top = carries[0] + bounds[0]
            for c, b in zip(carries[1:], bounds[1:]):
                top = jnp.maximum(top, c + b)
            return jnp.max(top)

        def cond(state):
            step, top = state[0], state[1]
            return (step < i) & (top > SB_SKIP)

        def k_block(state):
            step = state[0]
            accs, carries = list(state[2:2 + n_heads]), list(state[2 + n_heads:])
            k_rows = _rows((i - 1 - step) * t, t, t)
            for h in range(n_heads):
                z = _dot_nt(qms[h], k_ref[0, k_rows, lanes(h)])
                cs = _rev_cumsum(_neg_softplus(z), tri_ref[...])
                a = jnp.exp(z + cs + carries[h])
                accs[h] = accs[h] + _dot(a.astype(BF16), v_ref[0, k_rows, lanes(h)])
                carries[h] = carries[h] + cs[:, 0:1]
            return (step + 1, live(carries), *accs, *carries)

        state = lax.while_loop(cond, k_block, (jnp.int32(0), live(carries), *accs, *carries))
        accs = state[2:2 + n_heads]
        for p in range(2):
            o_ref[0, q_rows, p * PAIR:(p + 1) * PAIR] = jnp.where(
                masks[0], accs[2 * p], accs[2 * p + 1]).astype(o_ref.dtype)
        return 0

    lax.fori_loop(0, seq // t, q_block, 0)


def _mixer_b(proj, tri):
    b, seq, _ = proj.shape
    blk = COL_B // GROUP_W

    def col(j):
        return pl.BlockSpec((1, seq, GROUP_W), lambda i: (i, 0, blk + j))

    return pl.pallas_call(
        _mixer_b_kernel,
        grid=(b,),
        in_specs=[col(0), col(1), col(2), pl.BlockSpec(tri.shape, lambda i: (0, 0))],
        out_specs=pl.BlockSpec((1, seq, GROUP_W), lambda i: (i, 0, 0)),
        out_shape=jax.ShapeDtypeStruct((b, seq, GROUP_W), BF16),
        compiler_params=_cparams("parallel"),
        name="mixer_b_stickbreak",
    )(proj, proj, proj, tri)


C_TILES_PER_STEP = 5


def _mixer_c_kernel(sink_ref, q_ref, k_ref, v_ref, bias_ref, o_ref):
    seq = q_ref.shape[1]
    n_blocks = seq // BAND
    hm0 = _head_mask((BAND, PAIR), 0)

    for pb in range(2):
        lanes = slice(pb * PAIR, (pb + 1) * PAIR)

        def tile(q_start, k_start, n_keys, bias_cols, pb=pb, lanes=lanes):
            q_rows = _rows(q_start, BAND, BAND)
            k_rows = _rows(k_start, n_keys, BAND)
            q = q_ref[0, q_rows, lanes] * SCALE
            k = k_ref[0, k_rows, :]
            v = v_ref[0, k_rows, :]
            outs = []
            for hh in range(2):
                head = hh * 2 + pb
                sink = sink_ref[head]
                qm = jnp.where(_head_mask((BAND, PAIR), hh), q, 0.0).astype(BF16)
                s = _dot_nt(qm, k) + bias_ref[0, head, :, bias_cols]
                m = jnp.maximum(jnp.max(s, axis=-1, keepdims=True), sink)
                p = jnp.exp(s - m)
                den = jnp.sum(p, axis=-1, keepdims=True) + jnp.exp(sink - m)
                outs.append(_dot(p.astype(BF16), v) / den)
            o_ref[0, q_rows, lanes] = jnp.where(hm0, outs[0], outs[1]).astype(o_ref.dtype)

        tile(0, 0, BAND, slice(BAND, 2 * BAND))

        def later_step(g, _, tile=tile):
            for u in range(C_TILES_PER_STEP):
                q_start = (1 + g * C_TILES_PER_STEP + u) * BAND
                tile(q_start, q_start - BAND, 2 * BAND, slice(0, 2 * BAND))
            return 0

        lax.fori_loop(0, (n_blocks - 1) // C_TILES_PER_STEP, later_step, 0)


def _mixer_c(proj, bias_c, sinks):
    b, seq, _ = proj.shape
    return pl.pallas_call(
        _mixer_c_kernel,
        grid=(b,),
        in_specs=[pl.BlockSpec(memory_space=pltpu.SMEM),
                  pl.BlockSpec((1, seq, GROUP_W), lambda i: (i, 0, COL_CQ // GROUP_W)),
                  pl.BlockSpec((1, seq, PAIR), lambda i: (i, 0, COL_CK // PAIR)),
                  pl.BlockSpec((1, seq, PAIR), lambda i: (i, 0, COL_CV // PAIR)),
                  pl.BlockSpec(bias_c.shape, lambda i: (0, 0, 0, 0))],
        out_specs=pl.BlockSpec((1, seq, GROUP_W), lambda i: (i, 0, 0)),
        out_shape=jax.ShapeDtypeStruct((b, seq, GROUP_W), BF16),
        compiler_params=_cparams("parallel"),
        name="mixer_c_swa_sink",
    )(sinks, proj, proj, proj, bias_c)


def _mixer_d_kernel(q_ref, k_ref, v_ref, bias_ref, o_ref, sel_scr):
    seq = q_ref.shape[1]
    t = MOBA_BLOCK
    n_blk = seq // t
    n_heads = 4
    masks = [_head_mask((t, PAIR), hh) for hh in range(2)]
    blk_id = lax.broadcasted_iota(jnp.int32, (n_blk, t), 0)

    def lanes(h):
        return slice((h // 2) * PAIR, (h // 2 + 1) * PAIR)

    km_hi, km_lo = [], []
    for p in range(2):
        kmean = jnp.concatenate(
            [jnp.sum(k_ref[0, j * t:(j + 1) * t, p * PAIR:(p + 1) * PAIR].astype(F32), axis=0, keepdims=True)
             for j in range(n_blk)], axis=0) * (1.0 / t)
        hi = kmean.astype(BF16)
        km_hi.append(hi)
        km_lo.append((kmean - hi.astype(F32)).astype(BF16))

    def q_block(i, _):
        q_rows = _rows(i * t, t, t)
        qss, state = [], []
        for h in range(n_heads):
            q = q_ref[0, q_rows, lanes(h)]
            qm = jnp.where(masks[h % 2], q, 0.0).astype(BF16)
            gate = _dot_nt(km_hi[h // 2], qm) + _dot_nt(km_lo[h // 2], qm)
            gate = jnp.where(blk_id < i, gate, NEG)
            rank = jnp.zeros((n_blk, t), jnp.int32)
            for mth in range(n_blk):
                gm = gate[mth:mth + 1, :]
                ahead = (gm > gate) | ((gm == gate) & (mth < blk_id))
                rank = rank + ahead.astype(jnp.int32)
            chosen = (rank < MOBA_TOPK) & (blk_id < i)
            sel = jnp.where(chosen, 0.0, NEG).T
            for j in range(n_blk):
                sel_scr[h, j] = sel[:, j:j + 1]

            qs = qm * SCALE
            s = _dot_nt(qs, k_ref[0, q_rows, lanes(h)]) + bias_ref[0, h]
            m = jnp.max(s, axis=-1, keepdims=True)
            p = jnp.exp(s - m)
            l = jnp.sum(p, axis=-1, keepdims=True)
            qss.append(qs)
            state += [m, l, _dot(p.astype(BF16), v_ref[0, q_rows, lanes(h)])]

        def k_block(j, state):
            state = list(state)
            k_rows = _rows(j * t, t, t)
            for h in range(n_heads):
                m, l, acc = state[3 * h:3 * h + 3]
                s = _dot_nt(qss[h], k_ref[0, k_rows, lanes(h)]) + bias_ref[i - j, h] + sel_scr[h, j]
                m_new = jnp.maximum(m, jnp.max(s, axis=-1, keepdims=True))
                alpha = jnp.exp(m - m_new)
                p = jnp.exp(s - m_new)
                l = l * alpha + jnp.sum(p, axis=-1, keepdims=True)
                acc = acc * alpha + _dot(p.astype(BF16), v_ref[0, k_rows, lanes(h)])
                state[3 * h:3 * h + 3] = [m_new, l, acc]
            return tuple(state)

        state = lax.fori_loop(0, i, k_block, tuple(state))
        for p in range(2):
            o0 = state[6 * p + 2] / state[6 * p + 1]
            o1 = state[6 * p + 5] / state[6 * p + 4]
            o_ref[0, q_rows, p * PAIR:(p + 1) * PAIR] = jnp.where(masks[0], o0, o1).astype(o_ref.dtype)
        return 0

    lax.fori_loop(0, n_blk, q_block, 0)


def _mixer_d(proj, bias_d):
    b, seq, _ = proj.shape
    blk = COL_D // GROUP_W
    n_blk = seq // MOBA_BLOCK

    def col(j):
        return pl.BlockSpec((1, seq, GROUP_W), lambda i: (i, 0, blk + j))

    return pl.pallas_call(
        _mixer_d_kernel,
        grid=(b,),
        in_specs=[col(0), col(1), col(2),
                  pl.BlockSpec(bias_d.shape, lambda i: (0, 0, 0, 0))],
        out_specs=pl.BlockSpec((1, seq, GROUP_W), lambda i: (i, 0, 0)),
        out_shape=jax.ShapeDtypeStruct((b, seq, GROUP_W), BF16),
        scratch_shapes=[pltpu.VMEM((4, n_blk, MOBA_BLOCK, 1), F32)],
        compiler_params=_cparams("parallel"),
        name="mixer_d_moba",
    )(proj, proj, proj, bias_d)


def _mix_out_kernel(x_ref, ya_ref, yb_ref, yc_ref, yd_ref, g_ref, w_ref, o_ref):
    acc = x_ref[...]
    for gi, y_ref in enumerate((ya_ref, yb_ref, yc_ref, yd_ref)):
        cols = slice(gi * GROUP_W, (gi + 1) * GROUP_W)
        yn = _rms(y_ref[...].astype(F32), g_ref[:, cols]).astype(BF16)
        acc = acc + _dot(yn, w_ref[cols, :])
    o_ref[...] = acc


def _mix_out(x, ys, g, w, tm):
    t, d = x.shape
    y_spec = pl.BlockSpec((tm, GROUP_W), lambda i: (i, 0))
    return pl.pallas_call(
        _mix_out_kernel,
        grid=(t // tm,),
        in_specs=[pl.BlockSpec((tm, d), lambda i: (i, 0)), y_spec, y_spec, y_spec, y_spec,
                  pl.BlockSpec((1, 4 * GROUP_W), lambda i: (0, 0)),
                  pl.BlockSpec((4 * GROUP_W, d), lambda i: (0, 0))],
        out_specs=pl.BlockSpec((tm, d), lambda i: (i, 0)),
        out_shape=jax.ShapeDtypeStruct((t, d), F32),
        compiler_params=_cparams("parallel"),
        name="mix_out",
    )(x, *ys, g.reshape(1, -1), w)


def _cross_kernel(x_ref, g_ref, wq_ref, kv_ref, wo_ref, o_ref):
    x = x_ref[0]
    h = _rms(x, g_ref[...]).astype(BF16)
    q = (_dot(h, wq_ref[...]) * SCALE).astype(BF16)
    tm = x.shape[0]
    acc = x
    for pair in range(2):
        lanes = slice(pair * PAIR, (pair + 1) * PAIR)
        k = kv_ref[0, :, lanes]
        v = kv_ref[0, :, X_W + pair * PAIR:X_W + (pair + 1) * PAIR]
        outs = []
        for hh in range(2):
            qm = jnp.where(_head_mask((tm, PAIR), hh), q[:, lanes], 0.0).astype(BF16)
            s = _dot_nt(qm, k)
            m = jnp.max(s, axis=-1, keepdims=True)
            p = jnp.exp(s - m)
            l = jnp.sum(p, axis=-1, keepdims=True)
            outs.append(_dot(p.astype(BF16), v) / l)
        o = jnp.where(_head_mask((tm, PAIR), 0), outs[0], outs[1]).astype(BF16)
        acc = acc + _dot(o, wo_ref[lanes, :])
    o_ref[0] = acc


def _cross(x, g, wq, kv, wo, tm):
    b, seq, d = x.shape
    mem_len = kv.shape[1]
    return pl.pallas_call(
        _cross_kernel,
        grid=(b, seq // tm),
        in_specs=[pl.BlockSpec((1, tm, d), lambda i, j: (i, j, 0)),
                  pl.BlockSpec((1, d), lambda i, j: (0, 0)),
                  pl.BlockSpec((d, X_W), lambda i, j: (0, 0)),
                  pl.BlockSpec((1, mem_len, 2 * X_W), lambda i, j: (i, 0, 0)),
                  pl.BlockSpec((X_W, d), lambda i, j: (0, 0))],
        out_specs=pl.BlockSpec((1, tm, d), lambda i, j: (i, j, 0)),
        out_shape=jax.ShapeDtypeStruct((b, seq, d), F32),
        compiler_params=_cparams("parallel", "parallel"),
        name="cross_attention",
    )(x, g.reshape(1, d), wq, kv, wo)


def _mlp_kernel(x_ref, g_ref, wu_ref, wd_ref, gf_ref, o_ref, xn_scr, acc_scr, *, final_norm):
    j = pl.program_id(1)

    @pl.when(j == 0)
    def _():
        x = x_ref[...]
        xn_scr[...] = _rms(x, g_ref[...]).astype(BF16)
        acc_scr[...] = x

    h = _dot(xn_scr[...], wu_ref[...])
    h = jnp.square(jnp.maximum(h, 0.0)).astype(BF16)
    acc_scr[...] += _dot(h, wd_ref[...])

    @pl.when(j == pl.num_programs(1) - 1)
    def _():
        y = acc_scr[...]
        if final_norm:
            y = _rms(y, gf_ref[...])
        o_ref[...] = y


def _mlp(x, g, wu, wd, g_final, final_norm, tm, tf):
    t, d = x.shape
    ff = wu.shape[1]
    return pl.pallas_call(
        functools.partial(_mlp_kernel, final_norm=final_norm),
        grid=(t // tm, ff // tf),
        in_specs=[pl.BlockSpec((tm, d), lambda i, j: (i, 0)),
                  pl.BlockSpec((1, d), lambda i, j: (0, 0)),
                  pl.BlockSpec((d, tf), lambda i, j: (0, j)),
                  pl.BlockSpec((tf, d), lambda i, j: (j, 0)),
                  pl.BlockSpec((1, d), lambda i, j: (0, 0))],
        out_specs=pl.BlockSpec((tm, d), lambda i, j: (i, 0)),
        out_shape=jax.ShapeDtypeStruct((t, d), F32),
        scratch_shapes=[pltpu.VMEM((tm, d), BF16), pltpu.VMEM((tm, d), F32)],
        compiler_params=_cparams("parallel", "arbitrary"),
        name="mlp",
    )(x, g.reshape(1, d), wu, wd, g_final.reshape(1, d))


def _permute_c_heads(a, axis, lo):
    idx = np.arange(a.shape[axis])
    seg = np.concatenate([lo + h * HEAD_DIM + np.arange(HEAD_DIM) for h in C_HEAD_ORDER])
    idx[lo:lo + GROUP_W] = seg
    return jnp.take(a, jnp.asarray(idx), axis=axis)


def kernel(x, mem, rel_table, g_mix, w_in, g_group, sinks, w_out, g_cross, g_mem, w_xq, w_xkv, w_xo,
           g_mlp, w_up, w_down, g_final):
    b, seq, d = x.shape
    mem_len = mem.shape[1]
    depth = w_in.shape[0]
    tokens = b * seq

    bkt_a, bkt_c, bkt_d = _static_buckets()
    tab_t = rel_table.astype(F32).T
    bias_a = _rel_bias(tab_t, jnp.asarray(bkt_a), A_BIAS_LO)
    bias_c = _rel_bias(tab_t, jnp.asarray(bkt_c), C_BIAS_LO)
    bias_d = _rel_bias(tab_t, jnp.asarray(bkt_d), D_BIAS_LO)
    tri = jnp.asarray(np.tril(np.ones((SB_BLOCK, SB_BLOCK), np.float32)), dtype=BF16)

    c_out_lo = 2 * GROUP_W
    xf = x.reshape(tokens, d)
    mem_f = mem.reshape(b * mem_len, d)
    for l in range(depth):
        w_in_l = _permute_c_heads(w_in[l], 1, COL_CQ).astype(BF16)
        g_group_l = _permute_c_heads(g_group[l], 0, c_out_lo)
        w_out_l = _permute_c_heads(w_out[l], 0, c_out_lo).astype(BF16)

        proj = _norm_matmul(xf, g_mix[l], w_in_l, tm=512).reshape(b, seq, IN_WIDTH)
        ya = _mixer_a(proj, bias_a)
        yb = _mixer_b(proj, tri)
        yc = _mixer_c(proj, bias_c, sinks[l].astype(F32))
        yd = _mixer_d(proj, bias_d)
        ys = [y.reshape(tokens, GROUP_W) for y in (ya, yb, yc, yd)]
        xf = _mix_out(xf, ys, g_group_l, w_out_l, tm=512)

        kv = _norm_matmul(mem_f, g_mem[l], w_xkv[l].astype(BF16), tm=512).reshape(b, mem_len, 2 * X_W)
        xf = _cross(xf.reshape(b, seq, d), g_cross[l], w_xq[l].astype(BF16), kv,
                    w_xo[l].astype(BF16), tm=512).reshape(tokens, d)

        xf = _mlp(xf, g_mlp[l], w_up[l].astype(BF16), w_down[l].astype(BF16), g_final,
                  final_norm=(l == depth - 1), tm=1024, tf=512)
    return xf.reshape(b, seq, d)
```

```python
import functools
import math

import numpy as np
import jax
import jax.numpy as jnp
from jax import lax
from jax.experimental import pallas as pl
from jax.experimental.pallas import tpu as pltpu

F32 = jnp.float32
BF16 = jnp.bfloat16

D_MODEL = 1024
HEAD_DIM = 64
PAIR = 2 * HEAD_DIM
NORM_EPS = 1e-6
NEG = -1e30
SCALE = HEAD_DIM ** -0.5
A_CONFIGS = ((128, 1), (512, 4), (2048, 16))
BAND = 128
C_WINDOW = 128
SB_BLOCK = 256
SB_SKIP = -100.0
MOBA_BLOCK = 256
MOBA_TOPK = 3
REL_BUCKETS = 32
REL_MAX_DIST = 2048
A_BIAS_LO, C_BIAS_LO, D_BIAS_LO = 0, 4, 8
GROUP_W = 256
X_W = 256
D_FF = 4 * D_MODEL
COL_A, COL_B, COL_CQ, COL_CK, COL_CV, COL_D = 0, 768, 1536, 1792, 1920, 2048
IN_WIDTH = 2816
C_HEAD_ORDER = (0, 2, 1, 3)

VMEM_LIMIT = 48 * 1024 * 1024


def _cparams(*sem):
    return pltpu.CompilerParams(dimension_semantics=sem, vmem_limit_bytes=VMEM_LIMIT)


def _dot(a, b):
    return lax.dot_general(a, b, (((1,), (0,)), ((), ())), preferred_element_type=F32)


def _dot_nt(a, b):
    return lax.dot_general(a, b, (((1,), (1,)), ((), ())), preferred_element_type=F32)


def _head_mask(shape, hh):
    lane = lax.broadcasted_iota(jnp.int32, shape, len(shape) - 1)
    return (lane < HEAD_DIM) if hh == 0 else (lane >= HEAD_DIM)


def _rows(start, size, align):
    if isinstance(start, int):
        return pl.ds(start, size)
    return pl.ds(pl.multiple_of(start, align), size)


def _bucket_np(dist):
    max_exact = REL_BUCKETS // 2
    n = np.maximum(dist, 0)
    nf = np.maximum(n, 1).astype(np.float64)
    large = max_exact + (np.log(nf / max_exact) / math.log(REL_MAX_DIST / max_exact)
                         * (REL_BUCKETS - max_exact)).astype(np.int32)
    large = np.minimum(large, REL_BUCKETS - 1)
    return np.where(n < max_exact, n, large).astype(np.int32)


def _static_buckets():
    qi = np.arange(BAND)[:, None]
    ki = np.arange(2 * BAND)[None, :]
    dist = qi + BAND - ki
    bkt_a = []
    for window, dil in A_CONFIGS:
        valid = (dist >= 0) & (dist <= window // dil)
        bkt_a.append(np.where(valid, _bucket_np(dist * dil), -1))
    valid_c = (dist >= 0) & (dist < C_WINDOW)
    bkt_c = np.where(valid_c, _bucket_np(dist), -1)[None]
    qd = np.arange(MOBA_BLOCK)[None, :]
    kd = np.arange(MOBA_BLOCK)[:, None]
    bkt_d = []
    for delta in range(8):
        dd = delta * MOBA_BLOCK + qd - kd
        bkt_d.append(np.where(dd >= 0, _bucket_np(dd), -1))
    return (np.stack(bkt_a).astype(np.int32), bkt_c.astype(np.int32),
            np.stack(bkt_d).astype(np.int32))


def _rel_bias_kernel(tab_ref, bkt_ref, o_ref, *, head_lo):
    h = head_lo + pl.program_id(1)
    b = bkt_ref[0]
    acc = jnp.full(b.shape, NEG, F32)
    for k in range(REL_BUCKETS):
        acc = jnp.where(b == k, tab_ref[h, k], acc)
    o_ref[0, 0] = acc


def _rel_bias(tab_t, buckets, head_lo):
    n, r, c = buckets.shape
    return pl.pallas_call(
        functools.partial(_rel_bias_kernel, head_lo=head_lo),
        grid=(n, 4),
        in_specs=[pl.BlockSpec(memory_space=pltpu.SMEM),
                  pl.BlockSpec((1, r, c), lambda i, h: (i, 0, 0))],
        out_specs=pl.BlockSpec((1, 1, r, c), lambda i, h: (i, h, 0, 0)),
        out_shape=jax.ShapeDtypeStruct((n, 4, r, c), F32),
        compiler_params=_cparams("arbitrary", "arbitrary"),
        name="rel_bias",
    )(tab_t, buckets)


def _rms(x, g):
    ms = jnp.mean(x * x, axis=-1, keepdims=True)
    return x * lax.rsqrt(ms + NORM_EPS) * g


def _norm_matmul_kernel(x_ref, g_ref, w_ref, o_ref):
    xn = _rms(x_ref[...], g_ref[...]).astype(BF16)
    o_ref[...] = _dot(xn, w_ref[...]).astype(o_ref.dtype)


def _norm_matmul(x, g, w, tm):
    t, d = x.shape
    n = w.shape[1]
    return pl.pallas_call(
        _norm_matmul_kernel,
        grid=(t // tm,),
        in_specs=[pl.BlockSpec((tm, d), lambda i: (i, 0)),
                  pl.BlockSpec((1, d), lambda i: (0, 0)),
                  pl.BlockSpec((d, n), lambda i: (0, 0))],
        out_specs=pl.BlockSpec((tm, n), lambda i: (i, 0)),
        out_shape=jax.ShapeDtypeStruct((t, n), BF16),
        compiler_params=_cparams("parallel"),
        name="norm_matmul",
    )(x, g.reshape(1, d), w)


A_TILES_PER_STEP = {1: 5, 4: 4, 16: 4}


def _mixer_a_kernel(q_ref, k_ref, v_ref, bias_ref, o_ref, qf, kf, vf, *stats):
    seq = q_ref.shape[1]
    pair = pl.program_id(1)
    chunk = 256
    hm0 = _head_mask((BAND, PAIR), 0)

    def upcast(i, _):
        rows = _rows(i * chunk, chunk, chunk)
        qf[rows, :] = q_ref[0, rows, :].astype(F32) * SCALE
        kf[rows, :] = k_ref[0, rows, :].astype(F32)
        vf[rows, :] = v_ref[0, rows, :].astype(F32)
        return 0

    lax.fori_loop(0, seq // chunk, upcast, 0)

    for c, (_, dil) in enumerate(A_CONFIGS):
        so, sm, sl = stats[3 * c:3 * c + 3]
        n_blocks = seq // dil // BAND
        per_step = A_TILES_PER_STEP[dil]

        def rows(start, n, dil=dil):
            if dil == 1:
                return _rows(start, n, BAND)
            return pl.ds(start, n, stride=dil)

        def tiles(q_starts, back, n_keys, bias_cols, c=c, rows=rows, so=so, sm=sm, sl=sl):
            both = range(2)
            q_rows = [rows(qs, BAND) for qs in q_starts]
            k_rows = [rows(qs - back, n_keys) for qs in q_starts]
            ks = [kf[kr, :].astype(BF16) for kr in k_rows]
            vs = [vf[kr, :].astype(BF16) for kr in k_rows]
            qms = [[jnp.where(_head_mask((BAND, PAIR), hh), qf[qr, :], 0.0).astype(BF16) for hh in both]
                   for qr in q_rows]
            ss = [[_dot_nt(qms[u][hh], ks[u]) for hh in both] for u in range(len(q_starts))]
            ms, ls, ps = [], [], []
            for u in range(len(q_starts)):
                for hh in both:
                    s = ss[u][hh] + bias_ref[c, pair * 2 + hh, :, bias_cols]
                    m = jnp.max(s, axis=-1, keepdims=True)
                    p = jnp.exp(s - m)
                    ms.append(m)
                    ls.append(jnp.sum(p, axis=-1, keepdims=True))
                    ps.append(p.astype(BF16))
            os = [_dot(ps[2 * u + hh], vs[u]) for u in range(len(q_starts)) for hh in both]
            for u, qr in enumerate(q_rows):
                sm[qr, :] = jnp.where(hm0, ms[2 * u], ms[2 * u + 1])
                sl[qr, :] = jnp.where(hm0, ls[2 * u], ls[2 * u + 1])
                so[qr, :] = jnp.where(hm0, os[2 * u], os[2 * u + 1])

        first_cols = slice(BAND, 2 * BAND)
        all_cols = slice(0, 2 * BAND)
        span = dil * BAND

        if dil <= per_step:
            tiles(list(range(dil)), 0, BAND, first_cols)
        else:
            def first_step(g, _, tiles=tiles, per_step=per_step):
                tiles([g * per_step + u for u in range(per_step)], 0, BAND, first_cols)
                return 0

            lax.fori_loop(0, dil // per_step, first_step, 0)

        if n_blocks > 1:
            if dil == 1:
                def later_step(g, _, tiles=tiles, per_step=per_step, span=span):
                    tiles([(1 + g * per_step + u) * span for u in range(per_step)], span, 2 * BAND, all_cols)
                    return 0

                lax.fori_loop(0, (n_blocks - 1) // per_step, later_step, 0)
            else:
                def later_step(n, _, tiles=tiles, dil=dil, span=span):
                    tiles([r + n * span for r in range(dil)], span, 2 * BAND, all_cols)
                    return 0

                lax.fori_loop(1, n_blocks, later_step, 0)

    def merge(i, _):
        rows = _rows(i * BAND, BAND, BAND)
        ms = [stats[3 * c + 1][rows, :] for c in range(3)]
        mx = jnp.maximum(jnp.maximum(ms[0], ms[1]), ms[2])
        ws = [jnp.exp(m - mx) for m in ms]
        num = sum(w * stats[3 * c][rows, :] for c, w in enumerate(ws))
        den = sum(w * stats[3 * c + 2][rows, :] for c, w in enumerate(ws))
        o_ref[0, rows, :] = (num / den).astype(o_ref.dtype)
        return 0

    lax.fori_loop(0, seq // BAND, merge, 0)


def _mixer_a(proj, bias_a):
    b, seq, _ = proj.shape
    blk = COL_A // PAIR

    def col(j):
        return pl.BlockSpec((1, seq, PAIR), lambda i, p: (i, 0, blk + 2 * j + p))

    return pl.pallas_call(
        _mixer_a_kernel,
        grid=(b, 2),
        in_specs=[col(0), col(1), col(2),
                  pl.BlockSpec(bias_a.shape, lambda i, p: (0, 0, 0, 0))],
        out_specs=pl.BlockSpec((1, seq, PAIR), lambda i, p: (i, 0, p)),
        out_shape=jax.ShapeDtypeStruct((b, seq, GROUP_W), BF16),
        scratch_shapes=[pltpu.VMEM((seq, PAIR), F32)] * (3 + 3 * len(A_CONFIGS)),
        compiler_params=_cparams("parallel", "parallel"),
        name="mixer_a_dilated",
    )(proj, proj, proj, bias_a)


def _neg_softplus(z):
    return -(jnp.maximum(z, 0.0) + jnp.log(1.0 + jnp.exp(-jnp.abs(z))))


def _rev_cumsums(lms, tri):
    his = [lm.astype(BF16) for lm in lms]
    los = [(lm - hi.astype(F32)).astype(BF16) for lm, hi in zip(lms, his)]
    return [_dot(hi, tri) + _dot(lo, tri) for hi, lo in zip(his, los)]


def _mixer_b_kernel(q_ref, k_ref, v_ref, tri_ref, o_ref):
    seq = q_ref.shape[1]
    t = SB_BLOCK
    n_heads = 4
    row = lax.broadcasted_iota(jnp.int32, (t, t), 0)
    colv = lax.broadcasted_iota(jnp.int32, (t, t), 1)
    past = colv < row
    masks = [_head_mask((t, PAIR), hh) for hh in range(2)]

    def lanes(h):
        return slice((h // 2) * PAIR, (h // 2 + 1) * PAIR)

    k_max = []
    for h in range(n_heads):
        def norm_step(i, best, h=h):
            kk = k_ref[0, _rows(i * t, t, t), lanes(h)].astype(F32)
            sq = jnp.sum(jnp.where(masks[h % 2], kk * kk, 0.0), axis=-1, keepdims=True)
            return jnp.maximum(best, sq)

        sq_max = lax.fori_loop(0, seq // t, norm_step, jnp.zeros((t, 1), F32))
        k_max.append(jnp.sqrt(jnp.max(sq_max, axis=0, keepdims=True)))

    def q_block(i, _):
        q_rows = _rows(i * t, t, t)
        heads = range(n_heads)
        qms = [jnp.where(masks[h % 2], q_ref[0, q_rows, lanes(h)] * SCALE, 0.0).astype(BF16) for h in heads]
        bounds = []
        for h in heads:
            qf = qms[h].astype(F32)
            bounds.append(jnp.sqrt(jnp.sum(qf * qf, axis=-1, keepdims=True)) * k_max[h])
        zs = [_dot_nt(qms[h], k_ref[0, q_rows, lanes(h)]) for h in heads]
        css = _rev_cumsums([jnp.where(past, _neg_softplus(z), 0.0) for z in zs], tri_ref[...])
        aa = [jnp.where(past, jnp.exp(zs[h] + css[h]), 0.0).astype(BF16) for h in heads]
        accs = [_dot(aa[h], v_ref[0, q_rows, lanes(h)]) for h in heads]
        carries = [cs[:, 0:1] for cs in css]

        def live(carries):
            top = carries[0] + bounds[0]
            for c, b in zip(carries[1:], bounds[1:]):
                top = jnp.maximum(top, c + b)
            return jnp.max(top)

        def cond(state):
            step, top = state[0], state[1]
            return (step < i) & (top > SB_SKIP)

        def k_block(state):
            step = state[0]
            accs, carries = list(state[2:2 + n_heads]), list(state[2 + n_heads:])
            k_rows = _rows((i - 1 - step) * t, t, t)
            zs = [_dot_nt(qms[h], k_ref[0, k_rows, lanes(h)]) for h in heads]
            css = _rev_cumsums([_neg_softplus(z) for z in zs], tri_ref[...])
            aa = [jnp.exp(zs[h] + css[h] + carries[h]).astype(BF16) for h in heads]
            pvs = [_dot(aa[h], v_ref[0, k_rows, lanes(h)]) for h in heads]
            accs = [accs[h] + pvs[h] for h in heads]
            carries = [carries[h] + css[h][:, 0:1] for h in heads]
            return (step + 1, live(carries), *accs, *carries)

        state = lax.while_loop(cond, k_block, (jnp.int32(0), live(carries), *accs, *carries))
        accs = state[2:2 + n_heads]
        for p in range(2):
            o_ref[0, q_rows, p * PAIR:(p + 1) * PAIR] = jnp.where(
                masks[0], accs[2 * p], accs[2 * p + 1]).astype(o_ref.dtype)
        return 0

    lax.fori_loop(0, seq // t, q_block, 0)


def _mixer_b(proj, tri):
    b, seq, _ = proj.shape
    blk = COL_B // GROUP_W

    def col(j):
        return pl.BlockSpec((1, seq, GROUP_W), lambda i: (i, 0, blk + j))

    return pl.pallas_call(
        _mixer_b_kernel,
        grid=(b,),
        in_specs=[col(0), col(1), col(2), pl.BlockSpec(tri.shape, lambda i: (0, 0))],
        out_specs=pl.BlockSpec((1, seq, GROUP_W), lambda i: (i, 0, 0)),
        out_shape=jax.ShapeDtypeStruct((b, seq, GROUP_W), BF16),
        compiler_params=_cparams("parallel"),
        name="mixer_b_stickbreak",
    )(proj, proj, proj, tri)


C_TILES_PER_STEP = 5


def _mixer_c_kernel(sink_ref, q_ref, k_ref, v_ref, bias_ref, o_ref):
    seq = q_ref.shape[1]
    n_blocks = seq // BAND
    hm0 = _head_mask((BAND, PAIR), 0)

    for pb in range(2):
        lanes = slice(pb * PAIR, (pb + 1) * PAIR)

        def tiles(q_starts, back, n_keys, bias_cols, pb=pb, lanes=lanes):
            both = range(2)
            q_rows = [_rows(qs, BAND, BAND) for qs in q_starts]
            k_rows = [_rows(qs - back, n_keys, BAND) for qs in q_starts]
            qms = [[jnp.where(_head_mask((BAND, PAIR), hh), q_ref[0, qr, lanes] * SCALE, 0.0).astype(BF16)
                    for hh in both] for qr in q_rows]
            ss = [[_dot_nt(qms[u][hh], k_ref[0, k_rows[u], :]) for hh in both] for u in range(len(q_starts))]
            dens, ps = [], []
            for u in range(len(q_starts)):
                for hh in both:
                    head = hh * 2 + pb
                    sink = sink_ref[head]
                    s = ss[u][hh] + bias_ref[0, head, :, bias_cols]
                    m = jnp.maximum(jnp.max(s, axis=-1, keepdims=True), sink)
                    p = jnp.exp(s - m)
                    dens.append(jnp.sum(p, axis=-1, keepdims=True) + jnp.exp(sink - m))
                    ps.append(p.astype(BF16))
            os = [_dot(ps[2 * u + hh], v_ref[0, k_rows[u], :]) for u in range(len(q_starts)) for hh in both]
            for u, qr in enumerate(q_rows):
                o_ref[0, qr, lanes] = jnp.where(hm0, os[2 * u] / dens[2 * u],
                                                os[2 * u + 1] / dens[2 * u + 1]).astype(o_ref.dtype)

        tiles([0], 0, BAND, slice(BAND, 2 * BAND))

        def later_step(g, _, tiles=tiles):
            tiles([(1 + g * C_TILES_PER_STEP + u) * BAND for u in range(C_TILES_PER_STEP)],
                  BAND, 2 * BAND, slice(0, 2 * BAND))
            return 0

        lax.fori_loop(0, (n_blocks - 1) // C_TILES_PER_STEP, later_step, 0)


def _mixer_c(proj, bias_c, sinks):
    b, seq, _ = proj.shape
    return pl.pallas_call(
        _mixer_c_kernel,
        grid=(b,),
        in_specs=[pl.BlockSpec(memory_space=pltpu.SMEM),
                  pl.BlockSpec((1, seq, GROUP_W), lambda i: (i, 0, COL_CQ // GROUP_W)),
                  pl.BlockSpec((1, seq, PAIR), lambda i: (i, 0, COL_CK // PAIR)),
                  pl.BlockSpec((1, seq, PAIR), lambda i: (i, 0, COL_CV // PAIR)),
                  pl.BlockSpec(bias_c.shape, lambda i: (0, 0, 0, 0))],
        out_specs=pl.BlockSpec((1, seq, GROUP_W), lambda i: (i, 0, 0)),
        out_shape=jax.ShapeDtypeStruct((b, seq, GROUP_W), BF16),
        compiler_params=_cparams("parallel"),
        name="mixer_c_swa_sink",
    )(sinks, proj, proj, proj, bias_c)


def _mixer_d_kernel(q_ref, k_ref, v_ref, bias_ref, o_ref, vt_scr, sel_scr):
    seq = q_ref.shape[1]
    t = MOBA_BLOCK
    n_blk = seq // t
    n_heads = 4
    feat = lax.broadcasted_iota(jnp.int32, (PAIR, t), 0)
    fmasks = [feat < HEAD_DIM, feat >= HEAD_DIM]
    blk_id = lax.broadcasted_iota(jnp.int32, (n_blk, t), 0)

    def lanes(h):
        return slice((h // 2) * PAIR, (h // 2 + 1) * PAIR)

    km_hi, km_lo = [], []
    for p in range(2):
        kmean = jnp.concatenate(
            [jnp.sum(k_ref[0, j * t:(j + 1) * t, p * PAIR:(p + 1) * PAIR].astype(F32), axis=0, keepdims=True)
             for j in range(n_blk)], axis=0) * (1.0 / t)
        hi = kmean.astype(BF16)
        km_hi.append(hi)
        km_lo.append((kmean - hi.astype(F32)).astype(BF16))
        for j in range(n_blk):
            vt_scr[p, j] = v_ref[0, j * t:(j + 1) * t, p * PAIR:(p + 1) * PAIR].astype(F32).T.astype(BF16)

    def q_block(i, _):
        q_rows = _rows(i * t, t, t)
        heads = range(n_heads)
        qms = [jnp.where(fmasks[h % 2], q_ref[0, q_rows, lanes(h)].astype(F32).T, 0.0).astype(BF16)
               for h in heads]
        qts = [qm * SCALE for qm in qms]
        gates = [_dot(km_hi[h // 2], qms[h]) + _dot(km_lo[h // 2], qms[h]) for h in heads]
        ss = [_dot(k_ref[0, q_rows, lanes(h)], qts[h]) for h in heads]
        for h in heads:
            gate = jnp.where(blk_id < i, gates[h], NEG)
            rank = jnp.zeros((n_blk, t), jnp.int32)
            for mth in range(n_blk):
                gm = gate[mth:mth + 1, :]
                ahead = (gm > gate) | ((gm == gate) & (mth < blk_id))
                rank = rank + ahead.astype(jnp.int32)
            chosen = (rank < MOBA_TOPK) & (blk_id < i)
            sel_scr[h] = jnp.where(chosen, 0.0, NEG)
        ms, ls, ps = [], [], []
        for h in heads:
            s = ss[h] + bias_ref[0, h]
            m = jnp.max(s, axis=0, keepdims=True)
            p = jnp.exp(s - m)
            ms.append(m)
            ls.append(jnp.sum(p, axis=0, keepdims=True))
            ps.append(p.astype(BF16))
        accs = [_dot(vt_scr[h // 2, i], ps[h]) for h in heads]
        state = []
        for h in heads:
            state += [ms[h], ls[h], accs[h]]

        def k_block(j, state):
            k_rows = _rows(j * t, t, t)
            ss = [_dot(k_ref[0, k_rows, lanes(h)], qts[h]) for h in heads]
            out, alphas, ps = [], [], []
            for h in heads:
                m, l = state[3 * h], state[3 * h + 1]
                s = ss[h] + bias_ref[i - j, h] + sel_scr[h, pl.ds(j, 1), :]
                m_new = jnp.maximum(m, jnp.max(s, axis=0, keepdims=True))
                alpha = jnp.exp(m - m_new)
                p = jnp.exp(s - m_new)
                out.append([m_new, l * alpha + jnp.sum(p, axis=0, keepdims=True)])
                alphas.append(alpha)
                ps.append(p.astype(BF16))
            pvs = [_dot(vt_scr[h // 2, j], ps[h]) for h in heads]
            new_state = []
            for h in heads:
                new_state += out[h] + [state[3 * h + 2] * alphas[h] + pvs[h]]
            return tuple(new_state)

        state = lax.fori_loop(0, i, k_block, tuple(state))
        for p in range(2):
            o0 = state[6 * p + 2] / state[6 * p + 1]
            o1 = state[6 * p + 5] / state[6 * p + 4]
            o_ref[0, q_rows, p * PAIR:(p + 1) * PAIR] = jnp.where(fmasks[0], o0, o1).T.astype(o_ref.dtype)
        return 0

    lax.fori_loop(0, n_blk, q_block, 0)


def _mixer_d(proj, bias_d):
    b, seq, _ = proj.shape
    blk = COL_D // GROUP_W
    n_blk = seq // MOBA_BLOCK

    def col(j):
        return pl.BlockSpec((1, seq, GROUP_W), lambda i: (i, 0, blk + j))

    return pl.pallas_call(
        _mixer_d_kernel,
        grid=(b,),
        in_specs=[col(0), col(1), col(2),
                  pl.BlockSpec(bias_d.shape, lambda i: (0, 0, 0, 0))],
        out_specs=pl.BlockSpec((1, seq, GROUP_W), lambda i: (i, 0, 0)),
        out_shape=jax.ShapeDtypeStruct((b, seq, GROUP_W), BF16),
        scratch_shapes=[pltpu.VMEM((2, n_blk, PAIR, MOBA_BLOCK), BF16),
                        pltpu.VMEM((4, n_blk, MOBA_BLOCK), F32)],
        compiler_params=_cparams("parallel"),
        name="mixer_d_moba",
    )(proj, proj, proj, bias_d)


def _mix_out_kernel(x_ref, ya_ref, yb_ref, yc_ref, yd_ref, g_ref, w_ref, o_ref):
    acc = x_ref[...]
    for gi, y_ref in enumerate((ya_ref, yb_ref, yc_ref, yd_ref)):
        cols = slice(gi * GROUP_W, (gi + 1) * GROUP_W)
        yn = _rms(y_ref[...].astype(F32), g_ref[:, cols]).astype(BF16)
        acc = acc + _dot(yn, w_ref[cols, :])
    o_ref[...] = acc


def _mix_out(x, ys, g, w, tm):
    t, d = x.shape
    y_spec = pl.BlockSpec((tm, GROUP_W), lambda i: (i, 0))
    return pl.pallas_call(
        _mix_out_kernel,
        grid=(t // tm,),
        in_specs=[pl.BlockSpec((tm, d), lambda i: (i, 0)), y_spec, y_spec, y_spec, y_spec,
                  pl.BlockSpec((1, 4 * GROUP_W), lambda i: (0, 0)),
                  pl.BlockSpec((4 * GROUP_W, d), lambda i: (0, 0))],
        out_specs=pl.BlockSpec((tm, d), lambda i: (i, 0)),
        out_shape=jax.ShapeDtypeStruct((t, d), F32),
        compiler_params=_cparams("parallel"),
        name="mix_out",
    )(x, *ys, g.reshape(1, -1), w)


def _cross_kernel(x_ref, g_ref, wq_ref, kv_ref, wo_ref, o_ref):
    x = x_ref[0]
    h = _rms(x, g_ref[...]).astype(BF16)
    q = (_dot(h, wq_ref[...]) * SCALE).astype(BF16)
    tm = x.shape[0]
    heads = range(4)

    def lanes(h, lo=0):
        return slice(lo + (h // 2) * PAIR, lo + (h // 2 + 1) * PAIR)

    qms = [jnp.where(_head_mask((tm, PAIR), h % 2), q[:, lanes(h)], 0.0).astype(BF16) for h in heads]
    ss = [_dot_nt(qms[h], kv_ref[0, :, lanes(h)]) for h in heads]
    ls, ps = [], []
    for h in heads:
        m = jnp.max(ss[h], axis=-1, keepdims=True)
        p = jnp.exp(ss[h] - m)
        ls.append(jnp.sum(p, axis=-1, keepdims=True))
        ps.append(p.astype(BF16))
    pvs = [_dot(ps[h], kv_ref[0, :, lanes(h, X_W)]) for h in heads]
    os = [jnp.where(_head_mask((tm, PAIR), 0), pvs[2 * p] / ls[2 * p], pvs[2 * p + 1] / ls[2 * p + 1]).astype(BF16)
          for p in range(2)]
    o_ref[0] = x + _dot(os[0], wo_ref[0:PAIR, :]) + _dot(os[1], wo_ref[PAIR:2 * PAIR, :])


def _cross(x, g, wq, kv, wo, tm):
    b, seq, d = x.shape
    mem_len = kv.shape[1]
    return pl.pallas_call(
        _cross_kernel,
        grid=(b, seq // tm),
        in_specs=[pl.BlockSpec((1, tm, d), lambda i, j: (i, j, 0)),
                  pl.BlockSpec((1, d), lambda i, j: (0, 0)),
                  pl.BlockSpec((d, X_W), lambda i, j: (0, 0)),
                  pl.BlockSpec((1, mem_len, 2 * X_W), lambda i, j: (i, 0, 0)),
                  pl.BlockSpec((X_W, d), lambda i, j: (0, 0))],
        out_specs=pl.BlockSpec((1, tm, d), lambda i, j: (i, j, 0)),
        out_shape=jax.ShapeDtypeStruct((b, seq, d), F32),
        compiler_params=_cparams("parallel", "parallel"),
        name="cross_attention",
    )(x, g.reshape(1, d), wq, kv, wo)


def _mlp_kernel(x_ref, g_ref, wu_ref, wd_ref, gf_ref, o_ref, xn_scr, acc_scr, *, final_norm):
    j = pl.program_id(1)

    @pl.when(j == 0)
    def _():
        x = x_ref[...]
        xn_scr[...] = _rms(x, g_ref[...]).astype(BF16)
        acc_scr[...] = x

    h = _dot(xn_scr[...], wu_ref[...])
    h = jnp.square(jnp.maximum(h, 0.0)).astype(BF16)
    acc_scr[...] += _dot(h, wd_ref[...])

    @pl.when(j == pl.num_programs(1) - 1)
    def _():
        y = acc_scr[...]
        if final_norm:
            y = _rms(y, gf_ref[...])
        o_ref[...] = y


def _mlp(x, g, wu, wd, g_final, final_norm, tm, tf):
    t, d = x.shape
    ff = wu.shape[1]
    return pl.pallas_call(
        functools.partial(_mlp_kernel, final_norm=final_norm),
        grid=(t // tm, ff // tf),
        in_specs=[pl.BlockSpec((tm, d), lambda i, j: (i, 0)),
                  pl.BlockSpec((1, d), lambda i, j: (0, 0)),
                  pl.BlockSpec((d, tf), lambda i, j: (0, j)),
                  pl.BlockSpec((tf, d), lambda i, j: (j, 0)),
                  pl.BlockSpec((1, d), lambda i, j: (0, 0))],
        out_specs=pl.BlockSpec((tm, d), lambda i, j: (i, 0)),
        out_shape=jax.ShapeDtypeStruct((t, d), F32),
        scratch_shapes=[pltpu.VMEM((tm, d), BF16), pltpu.VMEM((tm, d), F32)],
        compiler_params=_cparams("parallel", "arbitrary"),
        name="mlp",
    )(x, g.reshape(1, d), wu, wd, g_final.reshape(1, d))


def _permute_c_heads(a, axis, lo):
    idx = np.arange(a.shape[axis])
    seg = np.concatenate([lo + h * HEAD_DIM + np.arange(HEAD_DIM) for h in C_HEAD_ORDER])
    idx[lo:lo + GROUP_W] = seg
    return jnp.take(a, jnp.asarray(idx), axis=axis)


def kernel(x, mem, rel_table, g_mix, w_in, g_group, sinks, w_out, g_cross, g_mem, w_xq, w_xkv, w_xo,
           g_mlp, w_up, w_down, g_final):
    b, seq, d = x.shape
    mem_len = mem.shape[1]
    depth = w_in.shape[0]
    tokens = b * seq

    bkt_a, bkt_c, bkt_d = _static_buckets()
    tab_t = rel_table.astype(F32).T
    bias_a = _rel_bias(tab_t, jnp.asarray(bkt_a), A_BIAS_LO)
    bias_c = _rel_bias(tab_t, jnp.asarray(bkt_c), C_BIAS_LO)
    bias_d = _rel_bias(tab_t, jnp.asarray(bkt_d), D_BIAS_LO)
    tri = jnp.asarray(np.tril(np.ones((SB_BLOCK, SB_BLOCK), np.float32)), dtype=BF16)

    c_out_lo = 2 * GROUP_W
    xf = x.reshape(tokens, d)
    mem_f = mem.reshape(b * mem_len, d)
    for l in range(depth):
        w_in_l = _permute_c_heads(w_in[l], 1, COL_CQ).astype(BF16)
        g_group_l = _permute_c_heads(g_group[l], 0, c_out_lo)
        w_out_l = _permute_c_heads(w_out[l], 0, c_out_lo).astype(BF16)

        proj = _norm_matmul(xf, g_mix[l], w_in_l, tm=512).reshape(b, seq, IN_WIDTH)
        ya = _mixer_a(proj, bias_a)
        yb = _mixer_b(proj, tri)
        yc = _mixer_c(proj, bias_c, sinks[l].astype(F32))
        yd = _mixer_d(proj, bias_d)
        ys = [y.reshape(tokens, GROUP_W) for y in (ya, yb, yc, yd)]
        xf = _mix_out(xf, ys, g_group_l, w_out_l, tm=512)

        kv = _norm_matmul(mem_f, g_mem[l], w_xkv[l].astype(BF16), tm=512).reshape(b, mem_len, 2 * X_W)
        xf = _cross(xf.reshape(b, seq, d), g_cross[l], w_xq[l].astype(BF16), kv,
                    w_xo[l].astype(BF16), tm=512).reshape(tokens, d)

        xf = _mlp(xf, g_mlp[l], w_up[l].astype(BF16), w_down[l].astype(BF16), g_final,
                  final_norm=(l == depth - 1), tm=1024, tf=512)
    return xf.reshape(b, seq, d)
```

```python
import functools
import math

import numpy as np
import jax
import jax.numpy as jnp
from jax import lax
from jax.experimental import pallas as pl
from jax.experimental.pallas import tpu as pltpu

F32 = jnp.float32
BF16 = jnp.bfloat16

D_MODEL = 1024
HEAD_DIM = 64
PAIR = 2 * HEAD_DIM
NORM_EPS = 1e-6
NEG = -1e30
SCALE = HEAD_DIM ** -0.5
A_CONFIGS = ((128, 1), (512, 4), (2048, 16))
BAND = 128
C_WINDOW = 128
SB_BLOCK = 256
SB_SKIP = -100.0
MOBA_BLOCK = 256
MOBA_TOPK = 3
REL_BUCKETS = 32
REL_MAX_DIST = 2048
A_BIAS_LO, C_BIAS_LO, D_BIAS_LO = 0, 4, 8
GROUP_W = 256
X_W = 256
D_FF = 4 * D_MODEL
COL_A, COL_B, COL_CQ, COL_CK, COL_CV, COL_D = 0, 768, 1536, 1792, 1920, 2048
IN_WIDTH = 2816
C_HEAD_ORDER = (0, 2, 1, 3)

VMEM_LIMIT = 48 * 1024 * 1024
POST_VMEM_LIMIT = 56 * 1024 * 1024


def _cparams(*sem):
    return pltpu.CompilerParams(dimension_semantics=sem, vmem_limit_bytes=VMEM_LIMIT)


def _dot(a, b):
    return lax.dot_general(a, b, (((1,), (0,)), ((), ())), preferred_element_type=F32)


def _dot_nt(a, b):
    return lax.dot_general(a, b, (((1,), (1,)), ((), ())), preferred_element_type=F32)


def _head_mask(shape, hh):
    lane = lax.broadcasted_iota(jnp.int32, shape, len(shape) - 1)
    return (lane < HEAD_DIM) if hh == 0 else (lane >= HEAD_DIM)


def _rows(start, size, align):
    if isinstance(start, int):
        return pl.ds(start, size)
    return pl.ds(pl.multiple_of(start, align), size)


def _bucket_np(dist):
    max_exact = REL_BUCKETS // 2
    n = np.maximum(dist, 0)
    nf = np.maximum(n, 1).astype(np.float64)
    large = max_exact + (np.log(nf / max_exact) / math.log(REL_MAX_DIST / max_exact)
                         * (REL_BUCKETS - max_exact)).astype(np.int32)
    large = np.minimum(large, REL_BUCKETS - 1)
    return np.where(n < max_exact, n, large).astype(np.int32)


def _static_buckets():
    qi = np.arange(BAND)[:, None]
    ki = np.arange(2 * BAND)[None, :]
    dist = qi + BAND - ki
    bkt_a = []
    for window, dil in A_CONFIGS:
        valid = (dist >= 0) & (dist <= window // dil)
        bkt_a.append(np.where(valid, _bucket_np(dist * dil), -1))
    valid_c = (dist >= 0) & (dist < C_WINDOW)
    bkt_c = np.where(valid_c, _bucket_np(dist), -1)[None]
    qd = np.arange(MOBA_BLOCK)[None, :]
    kd = np.arange(MOBA_BLOCK)[:, None]
    bkt_d = []
    for delta in range(8):
        dd = delta * MOBA_BLOCK + qd - kd
        bkt_d.append(np.where(dd >= 0, _bucket_np(dd), -1))
    return (np.stack(bkt_a).astype(np.int32), bkt_c.astype(np.int32),
            np.stack(bkt_d).astype(np.int32))


def _rel_bias_kernel(tab_ref, bkt_ref, o_ref, *, head_lo):
    h = head_lo + pl.program_id(1)
    b = bkt_ref[0]
    acc = jnp.full(b.shape, NEG, F32)
    for k in range(REL_BUCKETS):
        acc = jnp.where(b == k, tab_ref[h, k], acc)
    o_ref[0, 0] = acc


def _rel_bias(tab_t, buckets, head_lo):
    n, r, c = buckets.shape
    return pl.pallas_call(
        functools.partial(_rel_bias_kernel, head_lo=head_lo),
        grid=(n, 4),
        in_specs=[pl.BlockSpec(memory_space=pltpu.SMEM),
                  pl.BlockSpec((1, r, c), lambda i, h: (i, 0, 0))],
        out_specs=pl.BlockSpec((1, 1, r, c), lambda i, h: (i, h, 0, 0)),
        out_shape=jax.ShapeDtypeStruct((n, 4, r, c), F32),
        compiler_params=_cparams("arbitrary", "arbitrary"),
        name="rel_bias",
    )(tab_t, buckets)


def _rms(x, g):
    ms = jnp.mean(x * x, axis=-1, keepdims=True)
    return x * lax.rsqrt(ms + NORM_EPS) * g


def _norm_matmul_kernel(x_ref, g_ref, w_ref, o_ref):
    xn = _rms(x_ref[...], g_ref[...]).astype(BF16)
    o_ref[...] = _dot(xn, w_ref[...]).astype(o_ref.dtype)


def _norm_matmul(x, g, w, tm):
    t, d = x.shape
    n = w.shape[1]
    return pl.pallas_call(
        _norm_matmul_kernel,
        grid=(t // tm,),
        in_specs=[pl.BlockSpec((tm, d), lambda i: (i, 0)),
                  pl.BlockSpec((1, d), lambda i: (0, 0)),
                  pl.BlockSpec((d, n), lambda i: (0, 0))],
        out_specs=pl.BlockSpec((tm, n), lambda i: (i, 0)),
        out_shape=jax.ShapeDtypeStruct((t, n), BF16),
        compiler_params=_cparams("parallel"),
        name="norm_matmul",
    )(x, g.reshape(1, d), w)


A_TILES_PER_STEP = {1: 5, 4: 4, 16: 4}


def _mixer_a_kernel(q_ref, k_ref, v_ref, bias_ref, o_ref, qf, kf, vf, *stats):
    seq = q_ref.shape[1]
    pair = pl.program_id(1)
    chunk = 256
    hm0 = _head_mask((BAND, PAIR), 0)

    def upcast(i, _):
        rows = _rows(i * chunk, chunk, chunk)
        qf[rows, :] = q_ref[0, rows, :].astype(F32) * SCALE
        kf[rows, :] = k_ref[0, rows, :].astype(F32)
        vf[rows, :] = v_ref[0, rows, :].astype(F32)
        return 0

    lax.fori_loop(0, seq // chunk, upcast, 0)

    for c, (_, dil) in enumerate(A_CONFIGS):
        so, sm, sl = stats[3 * c:3 * c + 3]
        n_blocks = seq // dil // BAND
        per_step = A_TILES_PER_STEP[dil]

        def rows(start, n, dil=dil):
            if dil == 1:
                return _rows(start, n, BAND)
            return pl.ds(start, n, stride=dil)

        def tiles(q_starts, back, n_keys, bias_cols, c=c, rows=rows, so=so, sm=sm, sl=sl):
            both = range(2)
            q_rows = [rows(qs, BAND) for qs in q_starts]
            k_rows = [rows(qs - back, n_keys) for qs in q_starts]
            ks = [kf[kr, :].astype(BF16) for kr in k_rows]
            vs = [vf[kr, :].astype(BF16) for kr in k_rows]
            qms = [[jnp.where(_head_mask((BAND, PAIR), hh), qf[qr, :], 0.0).astype(BF16) for hh in both]
                   for qr in q_rows]
            ss = [[_dot_nt(qms[u][hh], ks[u]) for hh in both] for u in range(len(q_starts))]
            ms, ls, ps = [], [], []
            for u in range(len(q_starts)):
                for hh in both:
                    s = ss[u][hh] + bias_ref[c, pair * 2 + hh, :, bias_cols]
                    m = jnp.max(s, axis=-1, keepdims=True)
                    p = jnp.exp(s - m)
                    ms.append(m)
                    ls.append(jnp.sum(p, axis=-1, keepdims=True))
                    ps.append(p.astype(BF16))
            os = [_dot(ps[2 * u + hh], vs[u]) for u in range(len(q_starts)) for hh in both]
            for u, qr in enumerate(q_rows):
                sm[qr, :] = jnp.where(hm0, ms[2 * u], ms[2 * u + 1])
                sl[qr, :] = jnp.where(hm0, ls[2 * u], ls[2 * u + 1])
                so[qr, :] = jnp.where(hm0, os[2 * u], os[2 * u + 1])

        first_cols = slice(BAND, 2 * BAND)
        all_cols = slice(0, 2 * BAND)
        span = dil * BAND

        if dil <= per_step:
            tiles(list(range(dil)), 0, BAND, first_cols)
        else:
            def first_step(g, _, tiles=tiles, per_step=per_step):
                tiles([g * per_step + u for u in range(per_step)], 0, BAND, first_cols)
                return 0

            lax.fori_loop(0, dil // per_step, first_step, 0)

        if n_blocks > 1:
            if dil == 1:
                def later_step(g, _, tiles=tiles, per_step=per_step, span=span):
                    tiles([(1 + g * per_step + u) * span for u in range(per_step)], span, 2 * BAND, all_cols)
                    return 0

                lax.fori_loop(0, (n_blocks - 1) // per_step, later_step, 0)
            else:
                def later_step(n, _, tiles=tiles, dil=dil, span=span):
                    tiles([r + n * span for r in range(dil)], span, 2 * BAND, all_cols)
                    return 0

                lax.fori_loop(1, n_blocks, later_step, 0)

    def merge(i, _):
        rows = _rows(i * BAND, BAND, BAND)
        ms = [stats[3 * c + 1][rows, :] for c in range(3)]
        mx = jnp.maximum(jnp.maximum(ms[0], ms[1]), ms[2])
        ws = [jnp.exp(m - mx) for m in ms]
        num = sum(w * stats[3 * c][rows, :] for c, w in enumerate(ws))
        den = sum(w * stats[3 * c + 2][rows, :] for c, w in enumerate(ws))
        o_ref[0, rows, :] = (num / den).astype(o_ref.dtype)
        return 0

    lax.fori_loop(0, seq // BAND, merge, 0)


def _mixer_a(proj, bias_a):
    b, seq, _ = proj.shape
    blk = COL_A // PAIR

    def col(j):
        return pl.BlockSpec((1, seq, PAIR), lambda i, p: (i, 0, blk + 2 * j + p))

    return pl.pallas_call(
        _mixer_a_kernel,
        grid=(b, 2),
        in_specs=[col(0), col(1), col(2),
                  pl.BlockSpec(bias_a.shape, lambda i, p: (0, 0, 0, 0))],
        out_specs=pl.BlockSpec((1, seq, PAIR), lambda i, p: (i, 0, p)),
        out_shape=jax.ShapeDtypeStruct((b, seq, GROUP_W), BF16),
        scratch_shapes=[pltpu.VMEM((seq, PAIR), F32)] * (3 + 3 * len(A_CONFIGS)),
        compiler_params=_cparams("parallel", "parallel"),
        name="mixer_a_dilated",
    )(proj, proj, proj, bias_a)


def _neg_softplus(z):
    return -(jnp.maximum(z, 0.0) + jnp.log(1.0 + jnp.exp(-jnp.abs(z))))


def _rev_cumsums(lms, tri):
    his = [lm.astype(BF16) for lm in lms]
    los = [(lm - hi.astype(F32)).astype(BF16) for lm, hi in zip(lms, his)]
    return [_dot(jnp.concatenate([hi, lo], axis=1), tri) for hi, lo in zip(his, los)]


def _mixer_b_kernel(q_ref, k_ref, v_ref, tri_ref, o_ref):
    seq = q_ref.shape[1]
    t = SB_BLOCK
    n_heads = 4
    row = lax.broadcasted_iota(jnp.int32, (t, t), 0)
    colv = lax.broadcasted_iota(jnp.int32, (t, t), 1)
    past = colv < row
    masks = [_head_mask((t, PAIR), hh) for hh in range(2)]

    def lanes(h):
        return slice((h // 2) * PAIR, (h // 2 + 1) * PAIR)

    k_max = []
    for h in range(n_heads):
        def norm_step(i, best, h=h):
            kk = k_ref[0, _rows(i * t, t, t), lanes(h)].astype(F32)
            sq = jnp.sum(jnp.where(masks[h % 2], kk * kk, 0.0), axis=-1, keepdims=True)
            return jnp.maximum(best, sq)

        sq_max = lax.fori_loop(0, seq // t, norm_step, jnp.zeros((t, 1), F32))
        k_max.append(jnp.sqrt(jnp.max(sq_max, axis=0, keepdims=True)))

    def q_block(i, _):
        q_rows = _rows(i * t, t, t)
        heads = range(n_heads)
        qms = [jnp.where(masks[h % 2], q_ref[0, q_rows, lanes(h)] * SCALE, 0.0).astype(BF16) for h in heads]
        bounds = []
        for h in heads:
            qf = qms[h].astype(F32)
            bounds.append(jnp.sqrt(jnp.sum(qf * qf, axis=-1, keepdims=True)) * k_max[h])
        zs = [_dot_nt(qms[h], k_ref[0, q_rows, lanes(h)]) for h in heads]
        css = _rev_cumsums([jnp.where(past, _neg_softplus(z), 0.0) for z in zs], tri_ref[...])
        aa = [jnp.where(past, jnp.exp(zs[h] + css[h]), 0.0).astype(BF16) for h in heads]
        accs = [_dot(aa[h], v_ref[0, q_rows, lanes(h)]) for h in heads]
        carries = [cs[:, 0:1] for cs in css]

        def live(carries):
            top = carries[0] + bounds[0]
            for c, b in zip(carries[1:], bounds[1:]):
                top = jnp.maximum(top, c + b)
            return jnp.max(top)

        def cond(state):
            step, top = state[0], state[1]
            return (step < i) & (top > SB_SKIP)

        def k_block(state):
            step = state[0]
            accs, carries = list(state[2:2 + n_heads]), list(state[2 + n_heads:])
            k_rows = _rows((i - 1 - step) * t, t, t)
            zs = [_dot_nt(qms[h], k_ref[0, k_rows, lanes(h)]) for h in heads]
            css = _rev_cumsums([_neg_softplus(z) for z in zs], tri_ref[...])
            aa = [jnp.exp(zs[h] + css[h] + carries[h]).astype(BF16) for h in heads]
            pvs = [_dot(aa[h], v_ref[0, k_rows, lanes(h)]) for h in heads]
            accs = [accs[h] + pvs[h] for h in heads]
            carries = [carries[h] + css[h][:, 0:1] for h in heads]
            return (step + 1, live(carries), *accs, *carries)

        state = lax.while_loop(cond, k_block, (jnp.int32(0), live(carries), *accs, *carries))
        accs = state[2:2 + n_heads]
        for p in range(2):
            o_ref[0, q_rows, p * PAIR:(p + 1) * PAIR] = jnp.where(
                masks[0], accs[2 * p], accs[2 * p + 1]).astype(o_ref.dtype)
        return 0

    lax.fori_loop(0, seq // t, q_block, 0)


def _mixer_b(proj, tri):
    b, seq, _ = proj.shape
    blk = COL_B // GROUP_W

    def col(j):
        return pl.BlockSpec((1, seq, GROUP_W), lambda i: (i, 0, blk + j))

    return pl.pallas_call(
        _mixer_b_kernel,
        grid=(b,),
        in_specs=[col(0), col(1), col(2), pl.BlockSpec(tri.shape, lambda i: (0, 0))],
        out_specs=pl.BlockSpec((1, seq, GROUP_W), lambda i: (i, 0, 0)),
        out_shape=jax.ShapeDtypeStruct((b, seq, GROUP_W), BF16),
        compiler_params=_cparams("parallel"),
        name="mixer_b_stickbreak",
    )(proj, proj, proj, tri)


C_TILES_PER_STEP = 5


def _mixer_c_kernel(sink_ref, q_ref, k_ref, v_ref, bias_ref, o_ref):
    seq = q_ref.shape[1]
    n_blocks = seq // BAND
    hm0 = _head_mask((BAND, PAIR), 0)

    for pb in range(2):
        lanes = slice(pb * PAIR, (pb + 1) * PAIR)

        def tiles(q_starts, back, n_keys, bias_cols, pb=pb, lanes=lanes):
            both = range(2)
            q_rows = [_rows(qs, BAND, BAND) for qs in q_starts]
            k_rows = [_rows(qs - back, n_keys, BAND) for qs in q_starts]
            qms = [[jnp.where(_head_mask((BAND, PAIR), hh), q_ref[0, qr, lanes] * SCALE, 0.0).astype(BF16)
                    for hh in both] for qr in q_rows]
            ss = [[_dot_nt(qms[u][hh], k_ref[0, k_rows[u], :]) for hh in both] for u in range(len(q_starts))]
            dens, ps = [], []
            for u in range(len(q_starts)):
                for hh in both:
                    head = hh * 2 + pb
                    sink = sink_ref[head]
                    s = ss[u][hh] + bias_ref[0, head, :, bias_cols]
                    m = jnp.maximum(jnp.max(s, axis=-1, keepdims=True), sink)
                    p = jnp.exp(s - m)
                    dens.append(jnp.sum(p, axis=-1, keepdims=True) + jnp.exp(sink - m))
                    ps.append(p.astype(BF16))
            os = [_dot(ps[2 * u + hh], v_ref[0, k_rows[u], :]) for u in range(len(q_starts)) for hh in both]
            for u, qr in enumerate(q_rows):
                o_ref[0, qr, lanes] = jnp.where(hm0, os[2 * u] / dens[2 * u],
                                                os[2 * u + 1] / dens[2 * u + 1]).astype(o_ref.dtype)

        tiles([0], 0, BAND, slice(BAND, 2 * BAND))

        def later_step(g, _, tiles=tiles):
            tiles([(1 + g * C_TILES_PER_STEP + u) * BAND for u in range(C_TILES_PER_STEP)],
                  BAND, 2 * BAND, slice(0, 2 * BAND))
            return 0

        lax.fori_loop(0, (n_blocks - 1) // C_TILES_PER_STEP, later_step, 0)


def _mixer_c(proj, bias_c, sinks):
    b, seq, _ = proj.shape
    return pl.pallas_call(
        _mixer_c_kernel,
        grid=(b,),
        in_specs=[pl.BlockSpec(memory_space=pltpu.SMEM),
                  pl.BlockSpec((1, seq, GROUP_W), lambda i: (i, 0, COL_CQ // GROUP_W)),
                  pl.BlockSpec((1, seq, PAIR), lambda i: (i, 0, COL_CK // PAIR)),
                  pl.BlockSpec((1, seq, PAIR), lambda i: (i, 0, COL_CV // PAIR)),
                  pl.BlockSpec(bias_c.shape, lambda i: (0, 0, 0, 0))],
        out_specs=pl.BlockSpec((1, seq, GROUP_W), lambda i: (i, 0, 0)),
        out_shape=jax.ShapeDtypeStruct((b, seq, GROUP_W), BF16),
        compiler_params=_cparams("parallel"),
        name="mixer_c_swa_sink",
    )(sinks, proj, proj, proj, bias_c)


def _mixer_d_kernel(q_ref, k_ref, v_ref, bias_ref, o_ref, vt_scr, sel_scr):
    seq = q_ref.shape[1]
    t = MOBA_BLOCK
    n_blk = seq // t
    n_heads = 4
    feat = lax.broadcasted_iota(jnp.int32, (PAIR, t), 0)
    fmasks = [feat < HEAD_DIM, feat >= HEAD_DIM]
    blk_id = lax.broadcasted_iota(jnp.int32, (n_blk, t), 0)

    def lanes(h):
        return slice((h // 2) * PAIR, (h // 2 + 1) * PAIR)

    km_hi, km_lo = [], []
    for p in range(2):
        kmean = jnp.concatenate(
            [jnp.sum(k_ref[0, j * t:(j + 1) * t, p * PAIR:(p + 1) * PAIR].astype(F32), axis=0, keepdims=True)
             for j in range(n_blk)], axis=0) * (1.0 / t)
        hi = kmean.astype(BF16)
        km_hi.append(hi)
        km_lo.append((kmean - hi.astype(F32)).astype(BF16))
        for j in range(n_blk):
            vt_scr[p, j] = v_ref[0, j * t:(j + 1) * t, p * PAIR:(p + 1) * PAIR].astype(F32).T.astype(BF16)

    def q_block(i, _):
        q_rows = _rows(i * t, t, t)
        heads = range(n_heads)
        qms = [jnp.where(fmasks[h % 2], q_ref[0, q_rows, lanes(h)].astype(F32).T, 0.0).astype(BF16)
               for h in heads]
        qts = [qm * SCALE for qm in qms]
        gates = [_dot(km_hi[h // 2], qms[h]) + _dot(km_lo[h // 2], qms[h]) for h in heads]
        ss = [_dot(k_ref[0, q_rows, lanes(h)], qts[h]) for h in heads]
        for h in heads:
            gate = jnp.where(blk_id < i, gates[h], NEG)
            rank = jnp.zeros((n_blk, t), jnp.int32)
            for mth in range(n_blk):
                gm = gate[mth:mth + 1, :]
                ahead = (gm > gate) | ((gm == gate) & (mth < blk_id))
                rank = rank + ahead.astype(jnp.int32)
            chosen = (rank < MOBA_TOPK) & (blk_id < i)
            sel_scr[h] = jnp.where(chosen, 0.0, NEG)
        ms, ls, ps = [], [], []
        for h in heads:
            s = ss[h] + bias_ref[0, h]
            m = jnp.max(s, axis=0, keepdims=True)
            p = jnp.exp(s - m)
            ms.append(m)
            ls.append(jnp.sum(p, axis=0, keepdims=True))
            ps.append(p.astype(BF16))
        accs = [_dot(vt_scr[h // 2, i], ps[h]) for h in heads]
        state = []
        for h in heads:
            state += [ms[h], ls[h], accs[h]]

        def k_block(j, state):
            k_rows = _rows(j * t, t, t)
            ss = [_dot(k_ref[0, k_rows, lanes(h)], qts[h]) for h in heads]
            out, alphas, ps = [], [], []
            for h in heads:
                m, l = state[3 * h], state[3 * h + 1]
                s = ss[h] + bias_ref[i - j, h] + sel_scr[h, pl.ds(j, 1), :]
                m_new = jnp.maximum(m, jnp.max(s, axis=0, keepdims=True))
                alpha = jnp.exp(m - m_new)
                p = jnp.exp(s - m_new)
                out.append([m_new, l * alpha + jnp.sum(p, axis=0, keepdims=True)])
                alphas.append(alpha)
                ps.append(p.astype(BF16))
            pvs = [_dot(vt_scr[h // 2, j], ps[h]) for h in heads]
            new_state = []
            for h in heads:
                new_state += out[h] + [state[3 * h + 2] * alphas[h] + pvs[h]]
            return tuple(new_state)

        state = lax.fori_loop(0, i, k_block, tuple(state))
        for p in range(2):
            o0 = state[6 * p + 2] / state[6 * p + 1]
            o1 = state[6 * p + 5] / state[6 * p + 4]
            o_ref[0, q_rows, p * PAIR:(p + 1) * PAIR] = jnp.where(fmasks[0], o0, o1).T.astype(o_ref.dtype)
        return 0

    lax.fori_loop(0, n_blk, q_block, 0)


def _mixer_d(proj, bias_d):
    b, seq, _ = proj.shape
    blk = COL_D // GROUP_W
    n_blk = seq // MOBA_BLOCK

    def col(j):
        return pl.BlockSpec((1, seq, GROUP_W), lambda i: (i, 0, blk + j))

    return pl.pallas_call(
        _mixer_d_kernel,
        grid=(b,),
        in_specs=[col(0), col(1), col(2),
                  pl.BlockSpec(bias_d.shape, lambda i: (0, 0, 0, 0))],
        out_specs=pl.BlockSpec((1, seq, GROUP_W), lambda i: (i, 0, 0)),
        out_shape=jax.ShapeDtypeStruct((b, seq, GROUP_W), BF16),
        scratch_shapes=[pltpu.VMEM((2, n_blk, PAIR, MOBA_BLOCK), BF16),
                        pltpu.VMEM((4, n_blk, MOBA_BLOCK), F32)],
        compiler_params=_cparams("parallel"),
        name="mixer_d_moba",
    )(proj, proj, proj, bias_d)


def _mix_out_rows(x, ys, g_ref, w_ref):
    acc = x
    for gi, y in enumerate(ys):
        cols = slice(gi * GROUP_W, (gi + 1) * GROUP_W)
        yn = _rms(y.astype(F32), g_ref[:, cols]).astype(BF16)
        acc = acc + _dot(yn, w_ref[cols, :])
    return acc


def _cross_rows(x, g_ref, wq_ref, kv_ref, wo_ref):
    h = _rms(x, g_ref[...]).astype(BF16)
    q = (_dot(h, wq_ref[...]) * SCALE).astype(BF16)
    tm = x.shape[0]
    heads = range(4)

    def lanes(h, lo=0):
        return slice(lo + (h // 2) * PAIR, lo + (h // 2 + 1) * PAIR)

    qms = [jnp.where(_head_mask((tm, PAIR), h % 2), q[:, lanes(h)], 0.0).astype(BF16) for h in heads]
    ss = [_dot_nt(qms[h], kv_ref[:, lanes(h)]) for h in heads]
    ls, ps = [], []
    for h in heads:
        m = jnp.max(ss[h], axis=-1, keepdims=True)
        p = jnp.exp(ss[h] - m)
        ls.append(jnp.sum(p, axis=-1, keepdims=True))
        ps.append(p.astype(BF16))
    pvs = [_dot(ps[h], kv_ref[:, lanes(h, X_W)]) for h in heads]
    os = [jnp.where(_head_mask((tm, PAIR), 0), pvs[2 * p] / ls[2 * p], pvs[2 * p + 1] / ls[2 * p + 1]).astype(BF16)
          for p in range(2)]
    return x + _dot(os[0], wo_ref[0:PAIR, :]) + _dot(os[1], wo_ref[PAIR:2 * PAIR, :])


POST_CHUNK = 512


def _post_kernel(x_ref, ya_ref, yb_ref, yc_ref, yd_ref, gg_ref, wout_ref, gc_ref, wq_ref, kv_ref, wo_ref,
                 gm_ref, wu_ref, wd_ref, gf_ref, o_ref, xn_scr, acc_scr, *, final_norm):
    j = pl.program_id(1)

    @pl.when(j == 0)
    def _():
        def chunk(c, _):
            rows = _rows(c * POST_CHUNK, POST_CHUNK, POST_CHUNK)
            ys = [y_ref[rows, :] for y_ref in (ya_ref, yb_ref, yc_ref, yd_ref)]
            acc_scr[rows, :] = _mix_out_rows(x_ref[rows, :], ys, gg_ref, wout_ref)
            x = _cross_rows(acc_scr[rows, :], gc_ref, wq_ref, kv_ref, wo_ref)
            xn_scr[rows, :] = _rms(x, gm_ref[...]).astype(BF16)
            acc_scr[rows, :] = x
            return 0

        lax.fori_loop(0, x_ref.shape[0] // POST_CHUNK, chunk, 0)

    h = _dot(xn_scr[...], wu_ref[...])
    h = jnp.square(jnp.maximum(h, 0.0)).astype(BF16)
    acc_scr[...] += _dot(h, wd_ref[...])

    @pl.when(j == pl.num_programs(1) - 1)
    def _():
        y = acc_scr[...]
        if final_norm:
            y = _rms(y, gf_ref[...])
        o_ref[...] = y


def _post(x, ys, g_group, w_out, g_cross, wq, kv, wo, g_mlp, wu, wd, g_final, final_norm, seq, tm, tf):
    t, d = x.shape
    ff = wu.shape[1]
    mem_len = kv.shape[1]
    tiles_per_seq = seq // tm

    def const(shape):
        return pl.BlockSpec(shape, lambda i, j: (0,) * len(shape))

    y_spec = pl.BlockSpec((tm, GROUP_W), lambda i, j: (i, 0))
    return pl.pallas_call(
        functools.partial(_post_kernel, final_norm=final_norm),
        grid=(t // tm, ff // tf),
        in_specs=[pl.BlockSpec((tm, d), lambda i, j: (i, 0)), y_spec, y_spec, y_spec, y_spec,
                  const((1, 4 * GROUP_W)), const((4 * GROUP_W, d)),
                  const((1, d)), const((d, X_W)),
                  pl.BlockSpec((None, mem_len, 2 * X_W), lambda i, j: (i // tiles_per_seq, 0, 0)),
                  const((X_W, d)),
                  const((1, d)),
                  pl.BlockSpec((d, tf), lambda i, j: (0, j)),
                  pl.BlockSpec((tf, d), lambda i, j: (j, 0)),
                  const((1, d))],
        out_specs=pl.BlockSpec((tm, d), lambda i, j: (i, 0)),
        out_shape=jax.ShapeDtypeStruct((t, d), F32),
        scratch_shapes=[pltpu.VMEM((tm, d), BF16), pltpu.VMEM((tm, d), F32)],
        compiler_params=pltpu.CompilerParams(dimension_semantics=("parallel", "arbitrary"),
                                             vmem_limit_bytes=POST_VMEM_LIMIT),
        name="post_mixer_block",
    )(x, *ys, g_group.reshape(1, -1), w_out, g_cross.reshape(1, d), wq, kv, wo,
      g_mlp.reshape(1, d), wu, wd, g_final.reshape(1, d))


def _permute_c_heads(a, axis, lo):
    idx = np.arange(a.shape[axis])
    seg = np.concatenate([lo + h * HEAD_DIM + np.arange(HEAD_DIM) for h in C_HEAD_ORDER])
    idx[lo:lo + GROUP_W] = seg
    return jnp.take(a, jnp.asarray(idx), axis=axis)


def kernel(x, mem, rel_table, g_mix, w_in, g_group, sinks, w_out, g_cross, g_mem, w_xq, w_xkv, w_xo,
           g_mlp, w_up, w_down, g_final):
    b, seq, d = x.shape
    mem_len = mem.shape[1]
    depth = w_in.shape[0]
    tokens = b * seq

    bkt_a, bkt_c, bkt_d = _static_buckets()
    tab_t = rel_table.astype(F32).T
    bias_a = _rel_bias(tab_t, jnp.asarray(bkt_a), A_BIAS_LO)
    bias_c = _rel_bias(tab_t, jnp.asarray(bkt_c), C_BIAS_LO)
    bias_d = _rel_bias(tab_t, jnp.asarray(bkt_d), D_BIAS_LO)
    tri = np.tril(np.ones((SB_BLOCK, SB_BLOCK), np.float32))
    tri = jnp.asarray(np.concatenate([tri, tri], axis=0), dtype=BF16)

    c_out_lo = 2 * GROUP_W
    xf = x.reshape(tokens, d)
    mem_f = mem.reshape(b * mem_len, d)
    for l in range(depth):
        w_in_l = _permute_c_heads(w_in[l], 1, COL_CQ).astype(BF16)
        g_group_l = _permute_c_heads(g_group[l], 0, c_out_lo)
        w_out_l = _permute_c_heads(w_out[l], 0, c_out_lo).astype(BF16)

        proj = _norm_matmul(xf, g_mix[l], w_in_l, tm=512).reshape(b, seq, IN_WIDTH)
        ya = _mixer_a(proj, bias_a)
        yb = _mixer_b(proj, tri)
        yc = _mixer_c(proj, bias_c, sinks[l].astype(F32))
        yd = _mixer_d(proj, bias_d)
        ys = [y.reshape(tokens, GROUP_W) for y in (ya, yb, yc, yd)]
        kv = _norm_matmul(mem_f, g_mem[l], w_xkv[l].astype(BF16), tm=512).reshape(b, mem_len, 2 * X_W)
        xf = _post(xf, ys, g_group_l, w_out_l, g_cross[l], w_xq[l].astype(BF16), kv, w_xo[l].astype(BF16),
                   g_mlp[l], w_up[l].astype(BF16), w_down[l].astype(BF16), g_final,
                   final_norm=(l == depth - 1), seq=seq, tm=1024, tf=1024)
    return xf.reshape(b, seq, d)
```

```python
import functools
import math

import numpy as np
import jax
import jax.numpy as jnp
from jax import lax
from jax.experimental import pallas as pl
from jax.experimental.pallas import tpu as pltpu

F32 = jnp.float32
BF16 = jnp.bfloat16

D_MODEL = 1024
HEAD_DIM = 64
PAIR = 2 * HEAD_DIM
NORM_EPS = 1e-6
NEG = -1e30
SCALE = HEAD_DIM ** -0.5
A_CONFIGS = ((128, 1), (512, 4), (2048, 16))
BAND = 128
C_WINDOW = 128
SB_BLOCK = 256
SB_SKIP = -100.0
MOBA_BLOCK = 256
MOBA_TOPK = 3
REL_BUCKETS = 32
REL_MAX_DIST = 2048
A_BIAS_LO, C_BIAS_LO, D_BIAS_LO = 0, 4, 8
GROUP_W = 256
X_W = 256
D_FF = 4 * D_MODEL
COL_A, COL_B, COL_CQ, COL_CK, COL_CV, COL_D = 0, 768, 1536, 1792, 1920, 2048
IN_WIDTH = 2816
C_HEAD_ORDER = (0, 2, 1, 3)

VMEM_LIMIT = 48 * 1024 * 1024
POST_VMEM_LIMIT = 56 * 1024 * 1024


def _cparams(*sem):
    return pltpu.CompilerParams(dimension_semantics=sem, vmem_limit_bytes=VMEM_LIMIT)


def _dot(a, b):
    return lax.dot_general(a, b, (((1,), (0,)), ((), ())), preferred_element_type=F32)


def _dot_nt(a, b):
    return lax.dot_general(a, b, (((1,), (1,)), ((), ())), preferred_element_type=F32)


def _head_mask(shape, hh):
    lane = lax.broadcasted_iota(jnp.int32, shape, len(shape) - 1)
    return (lane < HEAD_DIM) if hh == 0 else (lane >= HEAD_DIM)


def _rows(start, size, align):
    if isinstance(start, int):
        return pl.ds(start, size)
    return pl.ds(pl.multiple_of(start, align), size)


def _bucket_np(dist):
    max_exact = REL_BUCKETS // 2
    n = np.maximum(dist, 0)
    nf = np.maximum(n, 1).astype(np.float64)
    large = max_exact + (np.log(nf / max_exact) / math.log(REL_MAX_DIST / max_exact)
                         * (REL_BUCKETS - max_exact)).astype(np.int32)
    large = np.minimum(large, REL_BUCKETS - 1)
    return np.where(n < max_exact, n, large).astype(np.int32)


def _static_buckets():
    qi = np.arange(BAND)[:, None]
    ki = np.arange(2 * BAND)[None, :]
    dist = qi + BAND - ki
    bkt_a = []
    for window, dil in A_CONFIGS:
        valid = (dist >= 0) & (dist <= window // dil)
        bkt_a.append(np.where(valid, _bucket_np(dist * dil), -1))
    valid_c = (dist >= 0) & (dist < C_WINDOW)
    bkt_c = np.where(valid_c, _bucket_np(dist), -1)[None]
    qd = np.arange(MOBA_BLOCK)[None, :]
    kd = np.arange(MOBA_BLOCK)[:, None]
    bkt_d = []
    for delta in range(8):
        dd = delta * MOBA_BLOCK + qd - kd
        bkt_d.append(np.where(dd >= 0, _bucket_np(dd), -1))
    return (np.stack(bkt_a).astype(np.int32), bkt_c.astype(np.int32),
            np.stack(bkt_d).astype(np.int32))


def _rel_bias_kernel(tab_ref, bkt_ref, o_ref, *, head_lo):
    h = head_lo + pl.program_id(1)
    b = bkt_ref[0]
    acc = jnp.full(b.shape, NEG, F32)
    for k in range(REL_BUCKETS):
        acc = jnp.where(b == k, tab_ref[h, k], acc)
    o_ref[0, 0] = acc


def _rel_bias(tab_t, buckets, head_lo):
    n, r, c = buckets.shape
    return pl.pallas_call(
        functools.partial(_rel_bias_kernel, head_lo=head_lo),
        grid=(n, 4),
        in_specs=[pl.BlockSpec(memory_space=pltpu.SMEM),
                  pl.BlockSpec((1, r, c), lambda i, h: (i, 0, 0))],
        out_specs=pl.BlockSpec((1, 1, r, c), lambda i, h: (i, h, 0, 0)),
        out_shape=jax.ShapeDtypeStruct((n, 4, r, c), F32),
        compiler_params=_cparams("arbitrary", "arbitrary"),
        name="rel_bias",
    )(tab_t, buckets)


def _rms(x, g):
    ms = jnp.mean(x * x, axis=-1, keepdims=True)
    return x * lax.rsqrt(ms + NORM_EPS) * g


def _norm_matmul_kernel(x_ref, g_ref, w_ref, o_ref):
    xn = _rms(x_ref[...], g_ref[...]).astype(BF16)
    o_ref[...] = _dot(xn, w_ref[...]).astype(o_ref.dtype)


def _norm_matmul(x, g, w, tm):
    t, d = x.shape
    n = w.shape[1]
    return pl.pallas_call(
        _norm_matmul_kernel,
        grid=(t // tm,),
        in_specs=[pl.BlockSpec((tm, d), lambda i: (i, 0)),
                  pl.BlockSpec((1, d), lambda i: (0, 0)),
                  pl.BlockSpec((d, n), lambda i: (0, 0))],
        out_specs=pl.BlockSpec((tm, n), lambda i: (i, 0)),
        out_shape=jax.ShapeDtypeStruct((t, n), BF16),
        compiler_params=_cparams("parallel"),
        name="norm_matmul",
    )(x, g.reshape(1, d), w)


A_TILES_PER_STEP = {1: 15, 4: 12, 16: 8}


def _mixer_a_kernel(q_ref, k_ref, v_ref, bias_ref, o_ref, qf, kf, vf, *stats):
    seq = q_ref.shape[1]
    pair = pl.program_id(1)
    chunk = 256
    hm0 = _head_mask((BAND, PAIR), 0)

    def upcast(i, _):
        rows = _rows(i * chunk, chunk, chunk)
        qf[rows, :] = q_ref[0, rows, :].astype(F32) * SCALE
        kf[rows, :] = k_ref[0, rows, :].astype(F32)
        vf[rows, :] = v_ref[0, rows, :].astype(F32)
        return 0

    lax.fori_loop(0, seq // chunk, upcast, 0)

    for c, (_, dil) in enumerate(A_CONFIGS):
        so, sm, sl = stats[3 * c:3 * c + 3]
        n_blocks = seq // dil // BAND
        per_step = A_TILES_PER_STEP[dil]

        def rows(start, n, dil=dil):
            if dil == 1:
                return _rows(start, n, BAND)
            return pl.ds(start, n, stride=dil)

        def tiles(q_starts, back, n_keys, bias_cols, c=c, rows=rows, so=so, sm=sm, sl=sl):
            both = range(2)
            q_rows = [rows(qs, BAND) for qs in q_starts]
            k_rows = [rows(qs - back, n_keys) for qs in q_starts]
            ks = [kf[kr, :].astype(BF16) for kr in k_rows]
            vs = [vf[kr, :].astype(BF16) for kr in k_rows]
            qms = [[jnp.where(_head_mask((BAND, PAIR), hh), qf[qr, :], 0.0).astype(BF16) for hh in both]
                   for qr in q_rows]
            ss = [[_dot_nt(qms[u][hh], ks[u]) for hh in both] for u in range(len(q_starts))]
            ms, ls, ps = [], [], []
            for u in range(len(q_starts)):
                for hh in both:
                    s = ss[u][hh] + bias_ref[c, pair * 2 + hh, :, bias_cols]
                    m = jnp.max(s, axis=-1, keepdims=True)
                    p = jnp.exp(s - m)
                    ms.append(m)
                    ls.append(jnp.sum(p, axis=-1, keepdims=True))
                    ps.append(p.astype(BF16))
            os = [_dot(ps[2 * u + hh], vs[u]) for u in range(len(q_starts)) for hh in both]
            for u, qr in enumerate(q_rows):
                sm[qr, :] = jnp.where(hm0, ms[2 * u], ms[2 * u + 1])
                sl[qr, :] = jnp.where(hm0, ls[2 * u], ls[2 * u + 1])
                so[qr, :] = jnp.where(hm0, os[2 * u], os[2 * u + 1])

        first_cols = slice(BAND, 2 * BAND)
        all_cols = slice(0, 2 * BAND)
        span = dil * BAND

        def sweep(total, per_step, q_start_of, *args, tiles=tiles):
            if total <= per_step:
                tiles([q_start_of(u) for u in range(total)], *args)
                return

            def step(g, _):
                tiles([q_start_of(g * per_step + u) for u in range(per_step)], *args)
                return 0

            lax.fori_loop(0, total // per_step, step, 0)

        def later_start(idx, dil=dil, span=span):
            r = idx & (dil - 1)
            n = 1 + (idx >> (dil.bit_length() - 1))
            return r + n * span

        sweep(dil, per_step, lambda idx: idx, 0, BAND, first_cols)
        if n_blocks > 1:
            sweep(dil * (n_blocks - 1), per_step, later_start, span, 2 * BAND, all_cols)

    def merge(i, _):
        rows = _rows(i * BAND, BAND, BAND)
        ms = [stats[3 * c + 1][rows, :] for c in range(3)]
        mx = jnp.maximum(jnp.maximum(ms[0], ms[1]), ms[2])
        ws = [jnp.exp(m - mx) for m in ms]
        num = sum(w * stats[3 * c][rows, :] for c, w in enumerate(ws))
        den = sum(w * stats[3 * c + 2][rows, :] for c, w in enumerate(ws))
        o_ref[0, rows, :] = (num / den).astype(o_ref.dtype)
        return 0

    lax.fori_loop(0, seq // BAND, merge, 0)


def _mixer_a(proj, bias_a):
    b, seq, _ = proj.shape
    blk = COL_A // PAIR

    def col(j):
        return pl.BlockSpec((1, seq, PAIR), lambda i, p: (i, 0, blk + 2 * j + p))

    return pl.pallas_call(
        _mixer_a_kernel,
        grid=(b, 2),
        in_specs=[col(0), col(1), col(2),
                  pl.BlockSpec(bias_a.shape, lambda i, p: (0, 0, 0, 0))],
        out_specs=pl.BlockSpec((1, seq, PAIR), lambda i, p: (i, 0, p)),
        out_shape=jax.ShapeDtypeStruct((b, seq, GROUP_W), BF16),
        scratch_shapes=[pltpu.VMEM((seq, PAIR), F32)] * (3 + 3 * len(A_CONFIGS)),
        compiler_params=_cparams("parallel", "parallel"),
        name="mixer_a_dilated",
    )(proj, proj, proj, bias_a)


LOG2E = math.log2(math.e)


def _softplus(z):
    return jnp.log(1.0 + jnp.exp2(jnp.abs(z) * (-LOG2E))) + jnp.maximum(z, 0.0)


def _rev_cumsums(lms, tri):
    his = [lm.astype(BF16) for lm in lms]
    los = [(lm - hi.astype(F32)).astype(BF16) for lm, hi in zip(lms, his)]
    return [_dot(jnp.concatenate([hi, lo], axis=1), tri) for hi, lo in zip(his, los)]


def _mixer_b_kernel(q_ref, k_ref, v_ref, tri_ref, o_ref):
    seq = q_ref.shape[1]
    t = SB_BLOCK
    n_heads = 4
    row = lax.broadcasted_iota(jnp.int32, (t, t), 0)
    colv = lax.broadcasted_iota(jnp.int32, (t, t), 1)
    past = colv < row
    masks = [_head_mask((t, PAIR), hh) for hh in range(2)]

    def lanes(h):
        return slice((h // 2) * PAIR, (h // 2 + 1) * PAIR)

    k_max = []
    for h in range(n_heads):
        def norm_step(i, best, h=h):
            kk = k_ref[0, _rows(i * t, t, t), lanes(h)].astype(F32)
            sq = jnp.sum(jnp.where(masks[h % 2], kk * kk, 0.0), axis=-1, keepdims=True)
            return jnp.maximum(best, sq)

        sq_max = lax.fori_loop(0, seq // t, norm_step, jnp.zeros((t, 1), F32))
        k_max.append(jnp.sqrt(jnp.max(sq_max, axis=0, keepdims=True)))

    def q_block(i, _):
        q_rows = _rows(i * t, t, t)
        heads = range(n_heads)
        qms = [jnp.where(masks[h % 2], q_ref[0, q_rows, lanes(h)] * SCALE, 0.0).astype(BF16) for h in heads]
        bounds = []
        for h in heads:
            qf = qms[h].astype(F32)
            bounds.append(jnp.sqrt(jnp.sum(qf * qf, axis=-1, keepdims=True)) * k_max[h])
        zs = [_dot_nt(qms[h], k_ref[0, q_rows, lanes(h)]) for h in heads]
        css = _rev_cumsums([jnp.where(past, _softplus(z), 0.0) for z in zs], tri_ref[...])
        aa = [jnp.where(past, jnp.exp(zs[h] - css[h]), 0.0).astype(BF16) for h in heads]
        accs = [_dot(aa[h], v_ref[0, q_rows, lanes(h)]) for h in heads]
        carries = [cs[:, 0:1] for cs in css]

        def walk(step0, rows, watch, accs, carries):
            qs = [qm[rows] for qm in qms]
            bs = [b[rows] for b in bounds]

            def live(carries):
                top = bs[0][watch] - carries[0][watch]
                for c, b in zip(carries[1:], bs[1:]):
                    top = jnp.maximum(top, b[watch] - c[watch])
                return jnp.max(top)

            def cond(state):
                step, top = state[0], state[1]
                return (step < i) & (top > SB_SKIP)

            def k_block(state):
                step = state[0]
                accs, carries = list(state[2:2 + n_heads]), list(state[2 + n_heads:])
                k_rows = _rows((i - 1 - step) * t, t, t)
                zs = [_dot_nt(qs[h], k_ref[0, k_rows, lanes(h)]) for h in heads]
                css = _rev_cumsums([_softplus(z) for z in zs], tri_ref[...])
                aa = [jnp.exp(zs[h] - css[h] - carries[h]).astype(BF16) for h in heads]
                pvs = [_dot(aa[h], v_ref[0, k_rows, lanes(h)]) for h in heads]
                accs = [accs[h] + pvs[h] for h in heads]
                carries = [carries[h] + css[h][:, 0:1] for h in heads]
                return (step + 1, live(carries), *accs, *carries)

            state = lax.while_loop(cond, k_block, (step0, live(carries), *accs, *carries))
            return state[0], list(state[2:2 + n_heads]), list(state[2 + n_heads:])

        upper, lower = slice(0, t // 2), slice(t // 2, t)
        step, accs, carries = walk(jnp.int32(0), slice(0, t), lower, accs, carries)
        _, accs_up, _ = walk(step, upper, upper, [a[upper] for a in accs], [c[upper] for c in carries])
        accs = [jnp.concatenate([accs_up[h], accs[h][lower]], axis=0) for h in heads]
        for p in range(2):
            o_ref[0, q_rows, p * PAIR:(p + 1) * PAIR] = jnp.where(
                masks[0], accs[2 * p], accs[2 * p + 1]).astype(o_ref.dtype)
        return 0

    lax.fori_loop(0, seq // t, q_block, 0)


def _mixer_b(proj, tri):
    b, seq, _ = proj.shape
    blk = COL_B // GROUP_W

    def col(j):
        return pl.BlockSpec((1, seq, GROUP_W), lambda i: (i, 0, blk + j))

    return pl.pallas_call(
        _mixer_b_kernel,
        grid=(b,),
        in_specs=[col(0), col(1), col(2), pl.BlockSpec(tri.shape, lambda i: (0, 0))],
        out_specs=pl.BlockSpec((1, seq, GROUP_W), lambda i: (i, 0, 0)),
        out_shape=jax.ShapeDtypeStruct((b, seq, GROUP_W), BF16),
        compiler_params=_cparams("parallel"),
        name="mixer_b_stickbreak",
    )(proj, proj, proj, tri)


C_TILES_PER_STEP = 5


def _mixer_c_kernel(sink_ref, q_ref, k_ref, v_ref, bias_ref, o_ref):
    seq = q_ref.shape[1]
    n_blocks = seq // BAND
    hm0 = _head_mask((BAND, PAIR), 0)

    for pb in range(2):
        lanes = slice(pb * PAIR, (pb + 1) * PAIR)

        def tiles(q_starts, back, n_keys, bias_cols, pb=pb, lanes=lanes):
            both = range(2)
            q_rows = [_rows(qs, BAND, BAND) for qs in q_starts]
            k_rows = [_rows(qs - back, n_keys, BAND) for qs in q_starts]
            qms = [[jnp.where(_head_mask((BAND, PAIR), hh), q_ref[0, qr, lanes] * SCALE, 0.0).astype(BF16)
                    for hh in both] for qr in q_rows]
            ss = [[_dot_nt(qms[u][hh], k_ref[0, k_rows[u], :]) for hh in both] for u in range(len(q_starts))]
            dens, ps = [], []
            for u in range(len(q_starts)):
                for hh in both:
                    head = hh * 2 + pb
                    sink = sink_ref[head]
                    s = ss[u][hh] + bias_ref[0, head, :, bias_cols]
                    m = jnp.maximum(jnp.max(s, axis=-1, keepdims=True), sink)
                    p = jnp.exp(s - m)
                    dens.append(jnp.sum(p, axis=-1, keepdims=True) + jnp.exp(sink - m))
                    ps.append(p.astype(BF16))
            os = [_dot(ps[2 * u + hh], v_ref[0, k_rows[u], :]) for u in range(len(q_starts)) for hh in both]
            for u, qr in enumerate(q_rows):
                o_ref[0, qr, lanes] = jnp.where(hm0, os[2 * u] / dens[2 * u],
                                                os[2 * u + 1] / dens[2 * u + 1]).astype(o_ref.dtype)

        tiles([0], 0, BAND, slice(BAND, 2 * BAND))

        def later_step(g, _, tiles=tiles):
            tiles([(1 + g * C_TILES_PER_STEP + u) * BAND for u in range(C_TILES_PER_STEP)],
                  BAND, 2 * BAND, slice(0, 2 * BAND))
            return 0

        lax.fori_loop(0, (n_blocks - 1) // C_TILES_PER_STEP, later_step, 0)


def _mixer_c(proj, bias_c, sinks):
    b, seq, _ = proj.shape
    return pl.pallas_call(
        _mixer_c_kernel,
        grid=(b,),
        in_specs=[pl.BlockSpec(memory_space=pltpu.SMEM),
                  pl.BlockSpec((1, seq, GROUP_W), lambda i: (i, 0, COL_CQ // GROUP_W)),
                  pl.BlockSpec((1, seq, PAIR), lambda i: (i, 0, COL_CK // PAIR)),
                  pl.BlockSpec((1, seq, PAIR), lambda i: (i, 0, COL_CV // PAIR)),
                  pl.BlockSpec(bias_c.shape, lambda i: (0, 0, 0, 0))],
        out_specs=pl.BlockSpec((1, seq, GROUP_W), lambda i: (i, 0, 0)),
        out_shape=jax.ShapeDtypeStruct((b, seq, GROUP_W), BF16),
        compiler_params=_cparams("parallel"),
        name="mixer_c_swa_sink",
    )(sinks, proj, proj, proj, bias_c)


def _mixer_d_kernel(q_ref, k_ref, v_ref, bias_ref, o_ref, vt_scr, sel_scr):
    seq = q_ref.shape[1]
    t = MOBA_BLOCK
    n_blk = seq // t
    n_heads = 4
    feat = lax.broadcasted_iota(jnp.int32, (PAIR, t), 0)
    fmasks = [feat < HEAD_DIM, feat >= HEAD_DIM]
    blk_id = lax.broadcasted_iota(jnp.int32, (n_blk, t), 0)

    def lanes(h):
        return slice((h // 2) * PAIR, (h // 2 + 1) * PAIR)

    km_hi, km_lo = [], []
    for p in range(2):
        kmean = jnp.concatenate(
            [jnp.sum(k_ref[0, j * t:(j + 1) * t, p * PAIR:(p + 1) * PAIR].astype(F32), axis=0, keepdims=True)
             for j in range(n_blk)], axis=0) * (1.0 / t)
        hi = kmean.astype(BF16)
        km_hi.append(hi)
        km_lo.append((kmean - hi.astype(F32)).astype(BF16))
        for j in range(n_blk):
            vt_scr[p, j] = v_ref[0, j * t:(j + 1) * t, p * PAIR:(p + 1) * PAIR].astype(F32).T.astype(BF16)

    def q_block(i, _):
        q_rows = _rows(i * t, t, t)
        heads = range(n_heads)
        qms = [jnp.where(fmasks[h % 2], q_ref[0, q_rows, lanes(h)].astype(F32).T, 0.0).astype(BF16)
               for h in heads]
        qts = [qm * SCALE for qm in qms]
        gates = [_dot(km_hi[h // 2], qms[h]) + _dot(km_lo[h // 2], qms[h]) for h in heads]
        ss = [_dot(k_ref[0, q_rows, lanes(h)], qts[h]) for h in heads]
        for h in heads:
            gate = jnp.where(blk_id < i, gates[h], NEG)
            rank = jnp.zeros((n_blk, t), jnp.int32)
            for mth in range(n_blk):
                gm = gate[mth:mth + 1, :]
                ahead = (gm > gate) | ((gm == gate) & (mth < blk_id))
                rank = rank + ahead.astype(jnp.int32)
            chosen = (rank < MOBA_TOPK) & (blk_id < i)
            sel_scr[h] = jnp.where(chosen, 0.0, NEG)
        ms, ls, ps = [], [], []
        for h in heads:
            s = ss[h] + bias_ref[0, h]
            m = jnp.max(s, axis=0, keepdims=True)
            p = jnp.exp(s - m)
            ms.append(m)
            ls.append(jnp.sum(p, axis=0, keepdims=True))
            ps.append(p.astype(BF16))
        accs = [_dot(vt_scr[h // 2, i], ps[h]) for h in heads]
        state = []
        for h in heads:
            state += [ms[h], ls[h], accs[h]]

        def k_blocks(js, state):
            k_rows = [_rows(j * t, t, t) for j in js]
            ss = [[_dot(k_ref[0, kr, lanes(h)], qts[h]) for kr in k_rows] for h in heads]
            out, alphas, ps = [], [], []
            for h in heads:
                m, l = state[3 * h], state[3 * h + 1]
                s = [ss[h][b] + bias_ref[i - j, h] + sel_scr[h, pl.ds(j, 1), :] for b, j in enumerate(js)]
                m_new = m
                for sb in s:
                    m_new = jnp.maximum(m_new, jnp.max(sb, axis=0, keepdims=True))
                alpha = jnp.exp(m - m_new)
                p = [jnp.exp(sb - m_new) for sb in s]
                out.append([m_new, l * alpha + sum(jnp.sum(pb, axis=0, keepdims=True) for pb in p)])
                alphas.append(alpha)
                ps.append(jnp.concatenate([pb.astype(BF16) for pb in p], axis=0))
            pvs = [_dot(jnp.concatenate([vt_scr[h // 2, j] for j in js], axis=1), ps[h]) for h in heads]
            new_state = []
            for h in heads:
                new_state += out[h] + [state[3 * h + 2] * alphas[h] + pvs[h]]
            return tuple(new_state)

        odd = i & 1
        state = lax.fori_loop(0, odd, lambda _, st: k_blocks([0], st), tuple(state))
        state = lax.fori_loop(0, i >> 1, lambda g, st: k_blocks([odd + 2 * g, odd + 2 * g + 1], st), state)
        for p in range(2):
            o0 = state[6 * p + 2] / state[6 * p + 1]
            o1 = state[6 * p + 5] / state[6 * p + 4]
            o_ref[0, q_rows, p * PAIR:(p + 1) * PAIR] = jnp.where(fmasks[0], o0, o1).T.astype(o_ref.dtype)
        return 0

    lax.fori_loop(0, n_blk, q_block, 0)


def _mixer_d(proj, bias_d):
    b, seq, _ = proj.shape
    blk = COL_D // GROUP_W
    n_blk = seq // MOBA_BLOCK

    def col(j):
        return pl.BlockSpec((1, seq, GROUP_W), lambda i: (i, 0, blk + j))

    return pl.pallas_call(
        _mixer_d_kernel,
        grid=(b,),
        in_specs=[col(0), col(1), col(2),
                  pl.BlockSpec(bias_d.shape, lambda i: (0, 0, 0, 0))],
        out_specs=pl.BlockSpec((1, seq, GROUP_W), lambda i: (i, 0, 0)),
        out_shape=jax.ShapeDtypeStruct((b, seq, GROUP_W), BF16),
        scratch_shapes=[pltpu.VMEM((2, n_blk, PAIR, MOBA_BLOCK), BF16),
                        pltpu.VMEM((4, n_blk, MOBA_BLOCK), F32)],
        compiler_params=_cparams("parallel"),
        name="mixer_d_moba",
    )(proj, proj, proj, bias_d)


def _mix_out_rows(x, ys, g_ref, w_ref):
    acc = x
    for gi, y in enumerate(ys):
        cols = slice(gi * GROUP_W, (gi + 1) * GROUP_W)
        yn = _rms(y.astype(F32), g_ref[:, cols]).astype(BF16)
        acc = acc + _dot(yn, w_ref[cols, :])
    return acc


def _cross_rows(x, g_ref, wq_ref, kv_ref, wo_ref):
    h = _rms(x, g_ref[...]).astype(BF16)
    q = (_dot(h, wq_ref[...]) * SCALE).astype(BF16)
    tm = x.shape[0]
    heads = range(4)

    def lanes(h, lo=0):
        return slice(lo + (h // 2) * PAIR, lo + (h // 2 + 1) * PAIR)

    qms = [jnp.where(_head_mask((tm, PAIR), h % 2), q[:, lanes(h)], 0.0).astype(BF16) for h in heads]
    ss = [_dot_nt(qms[h], kv_ref[:, lanes(h)]) for h in heads]
    ls, ps = [], []
    for h in heads:
        m = jnp.max(ss[h], axis=-1, keepdims=True)
        p = jnp.exp(ss[h] - m)
        ls.append(jnp.sum(p, axis=-1, keepdims=True))
        ps.append(p.astype(BF16))
    pvs = [_dot(ps[h], kv_ref[:, lanes(h, X_W)]) for h in heads]
    os = [jnp.where(_head_mask((tm, PAIR), 0), pvs[2 * p] / ls[2 * p], pvs[2 * p + 1] / ls[2 * p + 1]).astype(BF16)
          for p in range(2)]
    return x + _dot(os[0], wo_ref[0:PAIR, :]) + _dot(os[1], wo_ref[PAIR:2 * PAIR, :])


POST_CHUNK = 512


def _post_kernel(x_ref, ya_ref, yb_ref, yc_ref, yd_ref, gg_ref, wout_ref, gc_ref, wq_ref, kv_ref, wo_ref,
                 gm_ref, wu_ref, wd_ref, gf_ref, o_ref, xn_scr, acc_scr, *, final_norm):
    j = pl.program_id(1)

    @pl.when(j == 0)
    def _():
        def chunk(c, _):
            rows = _rows(c * POST_CHUNK, POST_CHUNK, POST_CHUNK)
            ys = [y_ref[rows, :] for y_ref in (ya_ref, yb_ref, yc_ref, yd_ref)]
            acc_scr[rows, :] = _mix_out_rows(x_ref[rows, :], ys, gg_ref, wout_ref)
            x = _cross_rows(acc_scr[rows, :], gc_ref, wq_ref, kv_ref, wo_ref)
            xn_scr[rows, :] = _rms(x, gm_ref[...]).astype(BF16)
            acc_scr[rows, :] = x
            return 0

        lax.fori_loop(0, x_ref.shape[0] // POST_CHUNK, chunk, 0)

    h = _dot(xn_scr[...], wu_ref[...])
    h = jnp.square(jnp.maximum(h, 0.0)).astype(BF16)
    acc_scr[...] += _dot(h, wd_ref[...])

    @pl.when(j == pl.num_programs(1) - 1)
    def _():
        y = acc_scr[...]
        if final_norm:
            y = _rms(y, gf_ref[...])
        o_ref[...] = y


def _post(x, ys, g_group, w_out, g_cross, wq, kv, wo, g_mlp, wu, wd, g_final, final_norm, seq, tm, tf):
    t, d = x.shape
    ff = wu.shape[1]
    mem_len = kv.shape[1]
    tiles_per_seq = seq // tm

    def const(shape):
        return pl.BlockSpec(shape, lambda i, j: (0,) * len(shape))

    y_spec = pl.BlockSpec((tm, GROUP_W), lambda i, j: (i, 0))
    return pl.pallas_call(
        functools.partial(_post_kernel, final_norm=final_norm),
        grid=(t // tm, ff // tf),
        in_specs=[pl.BlockSpec((tm, d), lambda i, j: (i, 0)), y_spec, y_spec, y_spec, y_spec,
                  const((1, 4 * GROUP_W)), const((4 * GROUP_W, d)),
                  const((1, d)), const((d, X_W)),
                  pl.BlockSpec((None, mem_len, 2 * X_W), lambda i, j: (i // tiles_per_seq, 0, 0)),
                  const((X_W, d)),
                  const((1, d)),
                  pl.BlockSpec((d, tf), lambda i, j: (0, j)),
                  pl.BlockSpec((tf, d), lambda i, j: (j, 0)),
                  const((1, d))],
        out_specs=pl.BlockSpec((tm, d), lambda i, j: (i, 0)),
        out_shape=jax.ShapeDtypeStruct((t, d), F32),
        scratch_shapes=[pltpu.VMEM((tm, d), BF16), pltpu.VMEM((tm, d), F32)],
        compiler_params=pltpu.CompilerParams(dimension_semantics=("parallel", "arbitrary"),
                                             vmem_limit_bytes=POST_VMEM_LIMIT),
        name="post_mixer_block",
    )(x, *ys, g_group.reshape(1, -1), w_out, g_cross.reshape(1, d), wq, kv, wo,
      g_mlp.reshape(1, d), wu, wd, g_final.reshape(1, d))


def _permute_c_heads(a, axis, lo):
    idx = np.arange(a.shape[axis])
    seg = np.concatenate([lo + h * HEAD_DIM + np.arange(HEAD_DIM) for h in C_HEAD_ORDER])
    idx[lo:lo + GROUP_W] = seg
    return jnp.take(a, jnp.asarray(idx), axis=axis)


def kernel(x, mem, rel_table, g_mix, w_in, g_group, sinks, w_out, g_cross, g_mem, w_xq, w_xkv, w_xo,
           g_mlp, w_up, w_down, g_final):
    b, seq, d = x.shape
    mem_len = mem.shape[1]
    depth = w_in.shape[0]
    tokens = b * seq

    bkt_a, bkt_c, bkt_d = _static_buckets()
    tab_t = rel_table.astype(F32).T
    bias_a = _rel_bias(tab_t, jnp.asarray(bkt_a), A_BIAS_LO)
    bias_c = _rel_bias(tab_t, jnp.asarray(bkt_c), C_BIAS_LO)
    bias_d = _rel_bias(tab_t, jnp.asarray(bkt_d), D_BIAS_LO)
    tri = np.tril(np.ones((SB_BLOCK, SB_BLOCK), np.float32))
    tri = jnp.asarray(np.concatenate([tri, tri], axis=0), dtype=BF16)

    c_out_lo = 2 * GROUP_W
    xf = x.reshape(tokens, d)
    mem_f = mem.reshape(b * mem_len, d)
    for l in range(depth):
        w_in_l = _permute_c_heads(w_in[l], 1, COL_CQ).astype(BF16)
        g_group_l = _permute_c_heads(g_group[l], 0, c_out_lo)
        w_out_l = _permute_c_heads(w_out[l], 0, c_out_lo).astype(BF16)

        proj = _norm_matmul(xf, g_mix[l], w_in_l, tm=512).reshape(b, seq, IN_WIDTH)
        ya = _mixer_a(proj, bias_a)
        yb = _mixer_b(proj, tri)
        yc = _mixer_c(proj, bias_c, sinks[l].astype(F32))
        yd = _mixer_d(proj, bias_d)
        ys = [y.reshape(tokens, GROUP_W) for y in (ya, yb, yc, yd)]
        kv = _norm_matmul(mem_f, g_mem[l], w_xkv[l].astype(BF16), tm=512).reshape(b, mem_len, 2 * X_W)
        xf = _post(xf, ys, g_group_l, w_out_l, g_cross[l], w_xq[l].astype(BF16), kv, w_xo[l].astype(BF16),
                   g_mlp[l], w_up[l].astype(BF16), w_down[l].astype(BF16), g_final,
                   final_norm=(l == depth - 1), seq=seq, tm=1024, tf=1024)
    return xf.reshape(b, seq, d)
```

```python
import functools
import math

import numpy as np
import jax
import jax.numpy as jnp
from jax import lax
from jax.experimental import pallas as pl
from jax.experimental.pallas import tpu as pltpu

F32 = jnp.float32
BF16 = jnp.bfloat16

D_MODEL = 1024
HEAD_DIM = 64
PAIR = 2 * HEAD_DIM
NORM_EPS = 1e-6
NEG = -1e30
SCALE = HEAD_DIM ** -0.5
A_CONFIGS = ((128, 1), (512, 4), (2048, 16))
BAND = 128
C_WINDOW = 128
SB_BLOCK = 256
SB_SKIP = -100.0
SB_UPPER = 176
MOBA_BLOCK = 256
MOBA_TOPK = 3
REL_BUCKETS = 32
REL_MAX_DIST = 2048
A_BIAS_LO, C_BIAS_LO, D_BIAS_LO = 0, 4, 8
GROUP_W = 256
X_W = 256
D_FF = 4 * D_MODEL
COL_A, COL_B, COL_CQ, COL_CK, COL_CV, COL_D = 0, 768, 1536, 1792, 1920, 2048
IN_WIDTH = 2816
C_HEAD_ORDER = (0, 2, 1, 3)

VMEM_LIMIT = 48 * 1024 * 1024
POST_VMEM_LIMIT = 56 * 1024 * 1024


def _cparams(*sem):
    return pltpu.CompilerParams(dimension_semantics=sem, vmem_limit_bytes=VMEM_LIMIT)


def _dot(a, b):
    return lax.dot_general(a, b, (((1,), (0,)), ((), ())), preferred_element_type=F32)


def _dot_nt(a, b):
    return lax.dot_general(a, b, (((1,), (1,)), ((), ())), preferred_element_type=F32)


def _head_mask(shape, hh):
    lane = lax.broadcasted_iota(jnp.int32, shape, len(shape) - 1)
    return (lane < HEAD_DIM) if hh == 0 else (lane >= HEAD_DIM)


def _rows(start, size, align):
    if isinstance(start, int):
        return pl.ds(start, size)
    return pl.ds(pl.multiple_of(start, align), size)


def _bucket_np(dist):
    max_exact = REL_BUCKETS // 2
    n = np.maximum(dist, 0)
    nf = np.maximum(n, 1).astype(np.float64)
    large = max_exact + (np.log(nf / max_exact) / math.log(REL_MAX_DIST / max_exact)
                         * (REL_BUCKETS - max_exact)).astype(np.int32)
    large = np.minimum(large, REL_BUCKETS - 1)
    return np.where(n < max_exact, n, large).astype(np.int32)


def _static_buckets():
    qi = np.arange(BAND)[:, None]
    ki = np.arange(2 * BAND)[None, :]
    dist = qi + BAND - ki
    bkt_a = []
    for window, dil in A_CONFIGS:
        valid = (dist >= 0) & (dist <= window // dil)
        bkt_a.append(np.where(valid, _bucket_np(dist * dil), -1))
    valid_c = (dist >= 0) & (dist < C_WINDOW)
    bkt_c = np.where(valid_c, _bucket_np(dist), -1)[None]
    qd = np.arange(MOBA_BLOCK)[None, :]
    kd = np.arange(MOBA_BLOCK)[:, None]
    bkt_d = []
    for delta in range(8):
        dd = delta * MOBA_BLOCK + qd - kd
        bkt_d.append(np.where(dd >= 0, _bucket_np(dd), -1))
    return (np.stack(bkt_a).astype(np.int32), bkt_c.astype(np.int32),
            np.stack(bkt_d).astype(np.int32))


def _rel_bias_kernel(tab_ref, bkt_ref, o_ref, *, head_lo):
    h = head_lo + pl.program_id(1)
    b = bkt_ref[0]
    acc = jnp.full(b.shape, NEG, F32)
    for k in range(REL_BUCKETS):
        acc = jnp.where(b == k, tab_ref[h, k], acc)
    o_ref[0, 0] = acc


def _rel_bias(tab_t, buckets, head_lo):
    n, r, c = buckets.shape
    return pl.pallas_call(
        functools.partial(_rel_bias_kernel, head_lo=head_lo),
        grid=(n, 4),
        in_specs=[pl.BlockSpec(memory_space=pltpu.SMEM),
                  pl.BlockSpec((1, r, c), lambda i, h: (i, 0, 0))],
        out_specs=pl.BlockSpec((1, 1, r, c), lambda i, h: (i, h, 0, 0)),
        out_shape=jax.ShapeDtypeStruct((n, 4, r, c), F32),
        compiler_params=_cparams("arbitrary", "arbitrary"),
        name="rel_bias",
    )(tab_t, buckets)


def _rms(x, g):
    ms = jnp.mean(x * x, axis=-1, keepdims=True)
    return x * lax.rsqrt(ms + NORM_EPS) * g


def _norm_matmul_kernel(x_ref, g_ref, w_ref, o_ref):
    xn = _rms(x_ref[...], g_ref[...]).astype(BF16)
    o_ref[...] = _dot(xn, w_ref[...]).astype(o_ref.dtype)


def _norm_matmul(x, g, w, tm):
    t, d = x.shape
    n = w.shape[1]
    return pl.pallas_call(
        _norm_matmul_kernel,
        grid=(t // tm,),
        in_specs=[pl.BlockSpec((tm, d), lambda i: (i, 0)),
                  pl.BlockSpec((1, d), lambda i: (0, 0)),
                  pl.BlockSpec((d, n), lambda i: (0, 0))],
        out_specs=pl.BlockSpec((tm, n), lambda i: (i, 0)),
        out_shape=jax.ShapeDtypeStruct((t, n), BF16),
        compiler_params=_cparams("parallel"),
        name="norm_matmul",
    )(x, g.reshape(1, d), w)


A_TILES_PER_STEP = {1: 15, 4: 12, 16: 8}


def _mixer_a_kernel(q_ref, k_ref, v_ref, bias_ref, o_ref, qf, kf, vf, *stats):
    seq = q_ref.shape[1]
    pair = pl.program_id(1)
    chunk = 256
    hm0 = _head_mask((BAND, PAIR), 0)

    def upcast(i, _):
        rows = _rows(i * chunk, chunk, chunk)
        qf[rows, :] = q_ref[0, rows, :].astype(F32) * SCALE
        kf[rows, :] = k_ref[0, rows, :].astype(F32)
        vf[rows, :] = v_ref[0, rows, :].astype(F32)
        return 0

    lax.fori_loop(0, seq // chunk, upcast, 0)

    for c, (_, dil) in enumerate(A_CONFIGS):
        so, sm, sl = stats[3 * c:3 * c + 3]
        n_blocks = seq // dil // BAND
        per_step = A_TILES_PER_STEP[dil]

        def rows(start, n, dil=dil):
            if dil == 1:
                return _rows(start, n, BAND)
            return pl.ds(start, n, stride=dil)

        def tiles(q_starts, back, n_keys, bias_cols, c=c, rows=rows, so=so, sm=sm, sl=sl):
            both = range(2)
            q_rows = [rows(qs, BAND) for qs in q_starts]
            k_rows = [rows(qs - back, n_keys) for qs in q_starts]
            ks = [kf[kr, :].astype(BF16) for kr in k_rows]
            vs = [vf[kr, :].astype(BF16) for kr in k_rows]
            qms = [[jnp.where(_head_mask((BAND, PAIR), hh), qf[qr, :], 0.0).astype(BF16) for hh in both]
                   for qr in q_rows]
            ss = [[_dot_nt(qms[u][hh], ks[u]) for hh in both] for u in range(len(q_starts))]
            ms, ls, ps = [], [], []
            for u in range(len(q_starts)):
                for hh in both:
                    s = ss[u][hh] + bias_ref[c, pair * 2 + hh, :, bias_cols]
                    m = jnp.max(s, axis=-1, keepdims=True)
                    p = jnp.exp(s - m)
                    ms.append(m)
                    ls.append(jnp.sum(p, axis=-1, keepdims=True))
                    ps.append(p.astype(BF16))
            os = [_dot(ps[2 * u + hh], vs[u]) for u in range(len(q_starts)) for hh in both]
            for u, qr in enumerate(q_rows):
                sm[qr, :] = jnp.where(hm0, ms[2 * u], ms[2 * u + 1])
                sl[qr, :] = jnp.where(hm0, ls[2 * u], ls[2 * u + 1])
                so[qr, :] = jnp.where(hm0, os[2 * u], os[2 * u + 1])

        first_cols = slice(BAND, 2 * BAND)
        all_cols = slice(0, 2 * BAND)
        span = dil * BAND

        def sweep(total, per_step, q_start_of, *args, tiles=tiles):
            if total <= per_step:
                tiles([q_start_of(u) for u in range(total)], *args)
                return

            def step(g, _):
                tiles([q_start_of(g * per_step + u) for u in range(per_step)], *args)
                return 0

            lax.fori_loop(0, total // per_step, step, 0)

        def later_start(idx, dil=dil, span=span):
            r = idx & (dil - 1)
            n = 1 + (idx >> (dil.bit_length() - 1))
            return r + n * span

        sweep(dil, per_step, lambda idx: idx, 0, BAND, first_cols)
        if n_blocks > 1:
            sweep(dil * (n_blocks - 1), per_step, later_start, span, 2 * BAND, all_cols)

    def merge(i, _):
        rows = _rows(i * BAND, BAND, BAND)
        ms = [stats[3 * c + 1][rows, :] for c in range(3)]
        mx = jnp.maximum(jnp.maximum(ms[0], ms[1]), ms[2])
        ws = [jnp.exp(m - mx) for m in ms]
        num = sum(w * stats[3 * c][rows, :] for c, w in enumerate(ws))
        den = sum(w * stats[3 * c + 2][rows, :] for c, w in enumerate(ws))
        o_ref[0, rows, :] = (num / den).astype(o_ref.dtype)
        return 0

    lax.fori_loop(0, seq // BAND, merge, 0)


def _mixer_a(proj, bias_a):
    b, seq, _ = proj.shape
    blk = COL_A // PAIR

    def col(j):
        return pl.BlockSpec((1, seq, PAIR), lambda i, p: (i, 0, blk + 2 * j + p))

    return pl.pallas_call(
        _mixer_a_kernel,
        grid=(b, 2),
        in_specs=[col(0), col(1), col(2),
                  pl.BlockSpec(bias_a.shape, lambda i, p: (0, 0, 0, 0))],
        out_specs=pl.BlockSpec((1, seq, PAIR), lambda i, p: (i, 0, p)),
        out_shape=jax.ShapeDtypeStruct((b, seq, GROUP_W), BF16),
        scratch_shapes=[pltpu.VMEM((seq, PAIR), F32)] * (3 + 3 * len(A_CONFIGS)),
        compiler_params=_cparams("parallel", "parallel"),
        name="mixer_a_dilated",
    )(proj, proj, proj, bias_a)


LOG2E = math.log2(math.e)


def _softplus(z):
    return jnp.log(1.0 + jnp.exp2(jnp.abs(z) * (-LOG2E))) + jnp.maximum(z, 0.0)


def _rev_cumsums(lms, tri):
    his = [lm.astype(BF16) for lm in lms]
    los = [(lm - hi.astype(F32)).astype(BF16) for lm, hi in zip(lms, his)]
    return [_dot(jnp.concatenate([hi, lo], axis=1), tri) for hi, lo in zip(his, los)]


def _mixer_b_kernel(q_ref, k_ref, v_ref, tri_ref, o_ref):
    seq = q_ref.shape[1]
    t = SB_BLOCK
    n_heads = 4
    row = lax.broadcasted_iota(jnp.int32, (t, t), 0)
    colv = lax.broadcasted_iota(jnp.int32, (t, t), 1)
    past = colv < row
    masks = [_head_mask((t, PAIR), hh) for hh in range(2)]

    def lanes(h):
        return slice((h // 2) * PAIR, (h // 2 + 1) * PAIR)

    k_max = []
    for h in range(n_heads):
        def norm_step(i, best, h=h):
            kk = k_ref[0, _rows(i * t, t, t), lanes(h)].astype(F32)
            sq = jnp.sum(jnp.where(masks[h % 2], kk * kk, 0.0), axis=-1, keepdims=True)
            return jnp.maximum(best, sq)

        sq_max = lax.fori_loop(0, seq // t, norm_step, jnp.zeros((t, 1), F32))
        k_max.append(jnp.sqrt(jnp.max(sq_max, axis=0, keepdims=True)))

    def q_block(i, _):
        q_rows = _rows(i * t, t, t)
        heads = range(n_heads)
        qms = [jnp.where(masks[h % 2], q_ref[0, q_rows, lanes(h)] * SCALE, 0.0).astype(BF16) for h in heads]
        bounds = []
        for h in heads:
            qf = qms[h].astype(F32)
            bounds.append(jnp.sqrt(jnp.sum(qf * qf, axis=-1, keepdims=True)) * k_max[h])
        zs = [_dot_nt(qms[h], k_ref[0, q_rows, lanes(h)]) for h in heads]
        css = _rev_cumsums([jnp.where(past, _softplus(z), 0.0) for z in zs], tri_ref[...])
        aa = [jnp.where(past, jnp.exp(zs[h] - css[h]), 0.0).astype(BF16) for h in heads]
        accs = [_dot(aa[h], v_ref[0, q_rows, lanes(h)]) for h in heads]
        carries = [cs[:, 0:1] for cs in css]

        def walk(step0, rows, watch, accs, carries):
            qs = [qm[rows] for qm in qms]
            bs = [b[rows] for b in bounds]

            def live(carries):
                top = bs[0][watch] - carries[0][watch]
                for c, b in zip(carries[1:], bs[1:]):
                    top = jnp.maximum(top, b[watch] - c[watch])
                return jnp.max(top)

            def cond(state):
                step, top = state[0], state[1]
                return (step < i) & (top > SB_SKIP)

            def k_block(state):
                step = state[0]
                accs, carries = list(state[2:2 + n_heads]), list(state[2 + n_heads:])
                k_rows = _rows((i - 1 - step) * t, t, t)
                zs = [_dot_nt(qs[h], k_ref[0, k_rows, lanes(h)]) for h in heads]
                css = _rev_cumsums([_softplus(z) for z in zs], tri_ref[...])
                aa = [jnp.exp(zs[h] - css[h] - carries[h]).astype(BF16) for h in heads]
                pvs = [_dot(aa[h], v_ref[0, k_rows, lanes(h)]) for h in heads]
                accs = [accs[h] + pvs[h] for h in heads]
                carries = [carries[h] + css[h][:, 0:1] for h in heads]
                return (step + 1, live(carries), *accs, *carries)

            state = lax.while_loop(cond, k_block, (step0, live(carries), *accs, *carries))
            return state[0], list(state[2:2 + n_heads]), list(state[2 + n_heads:])

        upper, lower = slice(0, SB_UPPER), slice(SB_UPPER, t)
        step, accs, carries = walk(jnp.int32(0), slice(0, t), lower, accs, carries)
        _, accs_up, _ = walk(step, upper, upper, [a[upper] for a in accs], [c[upper] for c in carries])
        accs = [jnp.concatenate([accs_up[h], accs[h][lower]], axis=0) for h in heads]
        for p in range(2):
            o_ref[0, q_rows, p * PAIR:(p + 1) * PAIR] = jnp.where(
                masks[0], accs[2 * p], accs[2 * p + 1]).astype(o_ref.dtype)
        return 0

    lax.fori_loop(0, seq // t, q_block, 0)


def _mixer_b(proj, tri):
    b, seq, _ = proj.shape
    blk = COL_B // GROUP_W

    def col(j):
        return pl.BlockSpec((1, seq, GROUP_W), lambda i: (i, 0, blk + j))

    return pl.pallas_call(
        _mixer_b_kernel,
        grid=(b,),
        in_specs=[col(0), col(1), col(2), pl.BlockSpec(tri.shape, lambda i: (0, 0))],
        out_specs=pl.BlockSpec((1, seq, GROUP_W), lambda i: (i, 0, 0)),
        out_shape=jax.ShapeDtypeStruct((b, seq, GROUP_W), BF16),
        compiler_params=_cparams("parallel"),
        name="mixer_b_stickbreak",
    )(proj, proj, proj, tri)


C_TILES_PER_STEP = 5


def _mixer_c_kernel(sink_ref, q_ref, k_ref, v_ref, bias_ref, o_ref):
    seq = q_ref.shape[1]
    n_blocks = seq // BAND
    hm0 = _head_mask((BAND, PAIR), 0)

    for pb in range(2):
        lanes = slice(pb * PAIR, (pb + 1) * PAIR)

        def tiles(q_starts, back, n_keys, bias_cols, pb=pb, lanes=lanes):
            both = range(2)
            q_rows = [_rows(qs, BAND, BAND) for qs in q_starts]
            k_rows = [_rows(qs - back, n_keys, BAND) for qs in q_starts]
            qms = [[jnp.where(_head_mask((BAND, PAIR), hh), q_ref[0, qr, lanes] * SCALE, 0.0).astype(BF16)
                    for hh in both] for qr in q_rows]
            ss = [[_dot_nt(qms[u][hh], k_ref[0, k_rows[u], :]) for hh in both] for u in range(len(q_starts))]
            dens, ps = [], []
            for u in range(len(q_starts)):
                for hh in both:
                    head = hh * 2 + pb
                    sink = sink_ref[head]
                    s = ss[u][hh] + bias_ref[0, head, :, bias_cols]
                    m = jnp.maximum(jnp.max(s, axis=-1, keepdims=True), sink)
                    p = jnp.exp(s - m)
                    dens.append(jnp.sum(p, axis=-1, keepdims=True) + jnp.exp(sink - m))
                    ps.append(p.astype(BF16))
            os = [_dot(ps[2 * u + hh], v_ref[0, k_rows[u], :]) for u in range(len(q_starts)) for hh in both]
            for u, qr in enumerate(q_rows):
                o_ref[0, qr, lanes] = jnp.where(hm0, os[2 * u] / dens[2 * u],
                                                os[2 * u + 1] / dens[2 * u + 1]).astype(o_ref.dtype)

        tiles([0], 0, BAND, slice(BAND, 2 * BAND))

        def later_step(g, _, tiles=tiles):
            tiles([(1 + g * C_TILES_PER_STEP + u) * BAND for u in range(C_TILES_PER_STEP)],
                  BAND, 2 * BAND, slice(0, 2 * BAND))
            return 0

        lax.fori_loop(0, (n_blocks - 1) // C_TILES_PER_STEP, later_step, 0)


def _mixer_c(proj, bias_c, sinks):
    b, seq, _ = proj.shape
    return pl.pallas_call(
        _mixer_c_kernel,
        grid=(b,),
        in_specs=[pl.BlockSpec(memory_space=pltpu.SMEM),
                  pl.BlockSpec((1, seq, GROUP_W), lambda i: (i, 0, COL_CQ // GROUP_W)),
                  pl.BlockSpec((1, seq, PAIR), lambda i: (i, 0, COL_CK // PAIR)),
                  pl.BlockSpec((1, seq, PAIR), lambda i: (i, 0, COL_CV // PAIR)),
                  pl.BlockSpec(bias_c.shape, lambda i: (0, 0, 0, 0))],
        out_specs=pl.BlockSpec((1, seq, GROUP_W), lambda i: (i, 0, 0)),
        out_shape=jax.ShapeDtypeStruct((b, seq, GROUP_W), BF16),
        compiler_params=_cparams("parallel"),
        name="mixer_c_swa_sink",
    )(sinks, proj, proj, proj, bias_c)


def _mixer_d_kernel(q_ref, k_ref, v_ref, bias_ref, o_ref, vt_scr, sel_scr):
    seq = q_ref.shape[1]
    t = MOBA_BLOCK
    n_blk = seq // t
    n_heads = 4
    feat = lax.broadcasted_iota(jnp.int32, (PAIR, t), 0)
    fmasks = [feat < HEAD_DIM, feat >= HEAD_DIM]
    blk_id = lax.broadcasted_iota(jnp.int32, (n_blk, t), 0)

    def lanes(h):
        return slice((h // 2) * PAIR, (h // 2 + 1) * PAIR)

    km_hi, km_lo = [], []
    for p in range(2):
        kmean = jnp.concatenate(
            [jnp.sum(k_ref[0, j * t:(j + 1) * t, p * PAIR:(p + 1) * PAIR].astype(F32), axis=0, keepdims=True)
             for j in range(n_blk)], axis=0) * (1.0 / t)
        hi = kmean.astype(BF16)
        km_hi.append(hi)
        km_lo.append((kmean - hi.astype(F32)).astype(BF16))
        for j in range(n_blk):
            vt_scr[p, j] = v_ref[0, j * t:(j + 1) * t, p * PAIR:(p + 1) * PAIR].astype(F32).T.astype(BF16)

    def q_block(i, _):
        q_rows = _rows(i * t, t, t)
        heads = range(n_heads)
        qms = [jnp.where(fmasks[h % 2], q_ref[0, q_rows, lanes(h)].astype(F32).T, 0.0).astype(BF16)
               for h in heads]
        qts = [qm * SCALE for qm in qms]
        gates = [_dot(km_hi[h // 2], qms[h]) + _dot(km_lo[h // 2], qms[h]) for h in heads]
        for h in heads:
            gate = jnp.where(blk_id < i, gates[h], NEG)
            rank = jnp.zeros((n_blk, t), jnp.int32)
            for mth in range(n_blk):
                gm = gate[mth:mth + 1, :]
                ahead = (gm > gate) | ((gm == gate) & (mth < blk_id))
                rank = rank + ahead.astype(jnp.int32)
            chosen = ((rank < MOBA_TOPK) & (blk_id < i)) | (blk_id == i)
            sel_scr[h] = jnp.where(chosen, 0.0, NEG)
        state = []
        for h in heads:
            state += [jnp.full((1, t), NEG, F32), jnp.zeros((1, t), F32), jnp.zeros((PAIR, t), F32)]

        def k_blocks(js, state):
            k_rows = [_rows(j * t, t, t) for j in js]
            ss = [[_dot(k_ref[0, kr, lanes(h)], qts[h]) for kr in k_rows] for h in heads]
            out, alphas, ps = [], [], []
            for h in heads:
                m, l = state[3 * h], state[3 * h + 1]
                s = [ss[h][b] + bias_ref[i - j, h] + sel_scr[h, pl.ds(j, 1), :] for b, j in enumerate(js)]
                m_new = m
                for sb in s:
                    m_new = jnp.maximum(m_new, jnp.max(sb, axis=0, keepdims=True))
                alpha = jnp.exp(m - m_new)
                p = [jnp.exp(sb - m_new) for sb in s]
                out.append([m_new, l * alpha + sum(jnp.sum(pb, axis=0, keepdims=True) for pb in p)])
                alphas.append(alpha)
                ps.append(jnp.concatenate([pb.astype(BF16) for pb in p], axis=0))
            pvs = [_dot(jnp.concatenate([vt_scr[h // 2, j] for j in js], axis=1), ps[h]) for h in heads]
            new_state = []
            for h in heads:
                new_state += out[h] + [state[3 * h + 2] * alphas[h] + pvs[h]]
            return tuple(new_state)

        def block_at(k):
            return jnp.where(k == 0, i, k - 1)

        odd = (i + 1) & 1
        state = lax.fori_loop(0, odd, lambda _, st: k_blocks([i], st), tuple(state))
        state = lax.fori_loop(0, (i + 1) >> 1,
                              lambda g, st: k_blocks([block_at(odd + 2 * g), block_at(odd + 2 * g + 1)], st), state)
        for p in range(2):
            o0 = state[6 * p + 2] / state[6 * p + 1]
            o1 = state[6 * p + 5] / state[6 * p + 4]
            o_ref[0, q_rows, p * PAIR:(p + 1) * PAIR] = jnp.where(fmasks[0], o0, o1).T.astype(o_ref.dtype)
        return 0

    lax.fori_loop(0, n_blk, q_block, 0)


def _mixer_d(proj, bias_d):
    b, seq, _ = proj.shape
    blk = COL_D // GROUP_W
    n_blk = seq // MOBA_BLOCK

    def col(j):
        return pl.BlockSpec((1, seq, GROUP_W), lambda i: (i, 0, blk + j))

    return pl.pallas_call(
        _mixer_d_kernel,
        grid=(b,),
        in_specs=[col(0), col(1), col(2),
                  pl.BlockSpec(bias_d.shape, lambda i: (0, 0, 0, 0))],
        out_specs=pl.BlockSpec((1, seq, GROUP_W), lambda i: (i, 0, 0)),
        out_shape=jax.ShapeDtypeStruct((b, seq, GROUP_W), BF16),
        scratch_shapes=[pltpu.VMEM((2, n_blk, PAIR, MOBA_BLOCK), BF16),
                        pltpu.VMEM((4, n_blk, MOBA_BLOCK), F32)],
        compiler_params=_cparams("parallel"),
        name="mixer_d_moba",
    )(proj, proj, proj, bias_d)


def _mix_out_rows(x, ys, g_ref, w_ref):
    acc = x
    for gi, y in enumerate(ys):
        cols = slice(gi * GROUP_W, (gi + 1) * GROUP_W)
        yn = _rms(y.astype(F32), g_ref[:, cols]).astype(BF16)
        acc = acc + _dot(yn, w_ref[cols, :])
    return acc


def _cross_rows(x, g_ref, wq_ref, kv_ref, wo_ref):
    h = _rms(x, g_ref[...]).astype(BF16)
    q = (_dot(h, wq_ref[...]) * SCALE).astype(BF16)
    tm = x.shape[0]
    heads = range(4)

    def lanes(h, lo=0):
        return slice(lo + (h // 2) * PAIR, lo + (h // 2 + 1) * PAIR)

    qms = [jnp.where(_head_mask((tm, PAIR), h % 2), q[:, lanes(h)], 0.0).astype(BF16) for h in heads]
    ss = [_dot_nt(qms[h], kv_ref[:, lanes(h)]) for h in heads]
    ls, ps = [], []
    for h in heads:
        m = jnp.max(ss[h], axis=-1, keepdims=True)
        p = jnp.exp(ss[h] - m)
        ls.append(jnp.sum(p, axis=-1, keepdims=True))
        ps.append(p.astype(BF16))
    pvs = [_dot(ps[h], kv_ref[:, lanes(h, X_W)]) for h in heads]
    os = [jnp.where(_head_mask((tm, PAIR), 0), pvs[2 * p] / ls[2 * p], pvs[2 * p + 1] / ls[2 * p + 1]).astype(BF16)
          for p in range(2)]
    return x + _dot(os[0], wo_ref[0:PAIR, :]) + _dot(os[1], wo_ref[PAIR:2 * PAIR, :])


POST_CHUNK = 512


def _post_kernel(x_ref, ya_ref, yb_ref, yc_ref, yd_ref, gg_ref, wout_ref, gc_ref, wq_ref, kv_ref, wo_ref,
                 gm_ref, wu_ref, wd_ref, gf_ref, o_ref, xn_scr, acc_scr, *, final_norm):
    j = pl.program_id(1)

    @pl.when(j == 0)
    def _():
        def chunk(c, _):
            rows = _rows(c * POST_CHUNK, POST_CHUNK, POST_CHUNK)
            ys = [y_ref[rows, :] for y_ref in (ya_ref, yb_ref, yc_ref, yd_ref)]
            acc_scr[rows, :] = _mix_out_rows(x_ref[rows, :], ys, gg_ref, wout_ref)
            x = _cross_rows(acc_scr[rows, :], gc_ref, wq_ref, kv_ref, wo_ref)
            xn_scr[rows, :] = _rms(x, gm_ref[...]).astype(BF16)
            acc_scr[rows, :] = x
            return 0

        lax.fori_loop(0, x_ref.shape[0] // POST_CHUNK, chunk, 0)

    h = _dot(xn_scr[...], wu_ref[...])
    h = jnp.square(jnp.maximum(h, 0.0)).astype(BF16)
    acc_scr[...] += _dot(h, wd_ref[...])

    @pl.when(j == pl.num_programs(1) - 1)
    def _():
        y = acc_scr[...]
        if final_norm:
            y = _rms(y, gf_ref[...])
        o_ref[...] = y


def _post(x, ys, g_group, w_out, g_cross, wq, kv, wo, g_mlp, wu, wd, g_final, final_norm, seq, tm, tf):
    t, d = x.shape
    ff = wu.shape[1]
    mem_len = kv.shape[1]
    tiles_per_seq = seq // tm

    def const(shape):
        return pl.BlockSpec(shape, lambda i, j: (0,) * len(shape))

    y_spec = pl.BlockSpec((tm, GROUP_W), lambda i, j: (i, 0))
    return pl.pallas_call(
        functools.partial(_post_kernel, final_norm=final_norm),
        grid=(t // tm, ff // tf),
        in_specs=[pl.BlockSpec((tm, d), lambda i, j: (i, 0)), y_spec, y_spec, y_spec, y_spec,
                  const((1, 4 * GROUP_W)), const((4 * GROUP_W, d)),
                  const((1, d)), const((d, X_W)),
                  pl.BlockSpec((None, mem_len, 2 * X_W), lambda i, j: (i // tiles_per_seq, 0, 0)),
                  const((X_W, d)),
                  const((1, d)),
                  pl.BlockSpec((d, tf), lambda i, j: (0, j)),
                  pl.BlockSpec((tf, d), lambda i, j: (j, 0)),
                  const((1, d))],
        out_specs=pl.BlockSpec((tm, d), lambda i, j: (i, 0)),
        out_shape=jax.ShapeDtypeStruct((t, d), F32),
        scratch_shapes=[pltpu.VMEM((tm, d), BF16), pltpu.VMEM((tm, d), F32)],
        compiler_params=pltpu.CompilerParams(dimension_semantics=("parallel", "arbitrary"),
                                             vmem_limit_bytes=POST_VMEM_LIMIT),
        name="post_mixer_block",
    )(x, *ys, g_group.reshape(1, -1), w_out, g_cross.reshape(1, d), wq, kv, wo,
      g_mlp.reshape(1, d), wu, wd, g_final.reshape(1, d))


def _permute_c_heads(a, axis, lo):
    idx = np.arange(a.shape[axis])
    seg = np.concatenate([lo + h * HEAD_DIM + np.arange(HEAD_DIM) for h in C_HEAD_ORDER])
    idx[lo:lo + GROUP_W] = seg
    return jnp.take(a, jnp.asarray(idx), axis=axis)


def kernel(x, mem, rel_table, g_mix, w_in, g_group, sinks, w_out, g_cross, g_mem, w_xq, w_xkv, w_xo,
           g_mlp, w_up, w_down, g_final):
    b, seq, d = x.shape
    mem_len = mem.shape[1]
    depth = w_in.shape[0]
    tokens = b * seq

    bkt_a, bkt_c, bkt_d = _static_buckets()
    tab_t = rel_table.astype(F32).T
    bias_a = _rel_bias(tab_t, jnp.asarray(bkt_a), A_BIAS_LO)
    bias_c = _rel_bias(tab_t, jnp.asarray(bkt_c), C_BIAS_LO)
    bias_d = _rel_bias(tab_t, jnp.asarray(bkt_d), D_BIAS_LO)
    tri = np.tril(np.ones((SB_BLOCK, SB_BLOCK), np.float32))
    tri = jnp.asarray(np.concatenate([tri, tri], axis=0), dtype=BF16)

    c_out_lo = 2 * GROUP_W
    xf = x.reshape(tokens, d)
    mem_f = mem.reshape(b * mem_len, d)
    for l in range(depth):
        w_in_l = _permute_c_heads(w_in[l], 1, COL_CQ).astype(BF16)
        g_group_l = _permute_c_heads(g_group[l], 0, c_out_lo)
        w_out_l = _permute_c_heads(w_out[l], 0, c_out_lo).astype(BF16)

        proj = _norm_matmul(xf, g_mix[l], w_in_l, tm=512).reshape(b, seq, IN_WIDTH)
        ya = _mixer_a(proj, bias_a)
        yb = _mixer_b(proj, tri)
        yc = _mixer_c(proj, bias_c, sinks[l].astype(F32))
        yd = _mixer_d(proj, bias_d)
        ys = [y.reshape(tokens, GROUP_W) for y in (ya, yb, yc, yd)]
        kv = _norm_matmul(mem_f, g_mem[l], w_xkv[l].astype(BF16), tm=512).reshape(b, mem_len, 2 * X_W)
        xf = _post(xf, ys, g_group_l, w_out_l, g_cross[l], w_xq[l].astype(BF16), kv, w_xo[l].astype(BF16),
                   g_mlp[l], w_up[l].astype(BF16), w_down[l].astype(BF16), g_final,
                   final_norm=(l == depth - 1), seq=seq, tm=1024, tf=1024)
    return xf.reshape(b, seq, d)
```

```python
import functools
import math

import numpy as np
import jax
import jax.numpy as jnp
from jax import lax
from jax.experimental import pallas as pl
from jax.experimental.pallas import tpu as pltpu

F32 = jnp.float32
BF16 = jnp.bfloat16

HEAD_DIM = 64
PAIR = 2 * HEAD_DIM
NORM_EPS = 1e-6
NEG = -1e30
SCALE = HEAD_DIM ** -0.5
A_CONFIGS = ((128, 1), (512, 4), (2048, 16))
BAND = 128
C_WINDOW = 128
SB_BLOCK = 256
SB_SKIP = -100.0
SB_UPPER = 176
MOBA_BLOCK = 256
MOBA_TOPK = 3
MOBA_SUM_ROWS = 16
REL_BUCKETS = 32
REL_MAX_DIST = 2048
A_BIAS_LO, C_BIAS_LO, D_BIAS_LO = 0, 4, 8
GROUP_W = 256
X_W = 256
COL_A, COL_B, COL_CQ, COL_CK, COL_CV, COL_D = 0, 768, 1536, 1792, 1920, 2048
IN_WIDTH = 2816
C_HEAD_ORDER = (0, 2, 1, 3)

VMEM_LIMIT = 48 * 1024 * 1024
POST_VMEM_LIMIT = 56 * 1024 * 1024


def _cparams(*sem):
    return pltpu.CompilerParams(dimension_semantics=sem, vmem_limit_bytes=VMEM_LIMIT)


def _dot(a, b):
    return lax.dot_general(a, b, (((1,), (0,)), ((), ())), preferred_element_type=F32)


def _dot_nt(a, b):
    return lax.dot_general(a, b, (((1,), (1,)), ((), ())), preferred_element_type=F32)


def _head_mask(shape, hh):
    lane = lax.broadcasted_iota(jnp.int32, shape, len(shape) - 1)
    return (lane < HEAD_DIM) if hh == 0 else (lane >= HEAD_DIM)


def _rows(start, size, align):
    if isinstance(start, int):
        return pl.ds(start, size)
    return pl.ds(pl.multiple_of(start, align), size)


def _bucket_np(dist):
    max_exact = REL_BUCKETS // 2
    n = np.maximum(dist, 0)
    nf = np.maximum(n, 1).astype(np.float64)
    large = max_exact + (np.log(nf / max_exact) / math.log(REL_MAX_DIST / max_exact)
                         * (REL_BUCKETS - max_exact)).astype(np.int32)
    large = np.minimum(large, REL_BUCKETS - 1)
    return np.where(n < max_exact, n, large).astype(np.int32)


def _static_buckets():
    qi = np.arange(BAND)[:, None]
    ki = np.arange(2 * BAND)[None, :]
    dist = qi + BAND - ki
    bkt_a = []
    for window, dil in A_CONFIGS:
        valid = (dist >= 0) & (dist <= window // dil)
        bkt_a.append(np.where(valid, _bucket_np(dist * dil), -1))
    valid_c = (dist >= 0) & (dist < C_WINDOW)
    bkt_c = np.where(valid_c, _bucket_np(dist), -1)[None]
    qd = np.arange(MOBA_BLOCK)[None, :]
    kd = np.arange(MOBA_BLOCK)[:, None]
    bkt_d = []
    for delta in range(8):
        dd = delta * MOBA_BLOCK + qd - kd
        bkt_d.append(np.where(dd >= 0, _bucket_np(dd), -1))
    return (np.stack(bkt_a).astype(np.int32), bkt_c.astype(np.int32),
            np.stack(bkt_d).astype(np.int32))


def _rel_bias_kernel(tab_ref, bkt_ref, o_ref, *, head_lo):
    h = head_lo + pl.program_id(1)
    b = bkt_ref[0]
    acc = jnp.full(b.shape, NEG, F32)
    for k in range(REL_BUCKETS):
        acc = jnp.where(b == k, tab_ref[h, k], acc)
    o_ref[0, 0] = acc


def _rel_bias(tab_t, buckets, head_lo):
    n, r, c = buckets.shape
    return pl.pallas_call(
        functools.partial(_rel_bias_kernel, head_lo=head_lo),
        grid=(n, 4),
        in_specs=[pl.BlockSpec(memory_space=pltpu.SMEM),
                  pl.BlockSpec((1, r, c), lambda i, h: (i, 0, 0))],
        out_specs=pl.BlockSpec((1, 1, r, c), lambda i, h: (i, h, 0, 0)),
        out_shape=jax.ShapeDtypeStruct((n, 4, r, c), F32),
        compiler_params=_cparams("arbitrary", "arbitrary"),
        name="rel_bias",
    )(tab_t, buckets)


def _rms(x, g):
    ms = jnp.mean(x * x, axis=-1, keepdims=True)
    return x * lax.rsqrt(ms + NORM_EPS) * g


def _norm_matmul_kernel(x_ref, g_ref, w_ref, o_ref):
    xn = _rms(x_ref[...], g_ref[...]).astype(BF16)
    o_ref[...] = _dot(xn, w_ref[...]).astype(o_ref.dtype)


def _norm_matmul(x, g, w, tm):
    t, d = x.shape
    n = w.shape[1]
    return pl.pallas_call(
        _norm_matmul_kernel,
        grid=(t // tm,),
        in_specs=[pl.BlockSpec((tm, d), lambda i: (i, 0)),
                  pl.BlockSpec((1, d), lambda i: (0, 0)),
                  pl.BlockSpec((d, n), lambda i: (0, 0))],
        out_specs=pl.BlockSpec((tm, n), lambda i: (i, 0)),
        out_shape=jax.ShapeDtypeStruct((t, n), BF16),
        compiler_params=_cparams("parallel"),
        name="norm_matmul",
    )(x, g.reshape(1, d), w)


A_TILES_PER_STEP = {1: 15, 4: 12, 16: 8}


def _mixer_a_kernel(q_ref, k_ref, v_ref, bias_ref, o_ref, qf, kf, vf, *stats):
    seq = q_ref.shape[1]
    pair = pl.program_id(1)
    chunk = 256
    hm0 = _head_mask((BAND, PAIR), 0)

    def upcast(i, _):
        rows = _rows(i * chunk, chunk, chunk)
        qf[rows, :] = q_ref[0, rows, :].astype(F32) * SCALE
        kf[rows, :] = k_ref[0, rows, :].astype(F32)
        vf[rows, :] = v_ref[0, rows, :].astype(F32)
        return 0

    lax.fori_loop(0, seq // chunk, upcast, 0)

    for c, (_, dil) in enumerate(A_CONFIGS):
        so, sm, sl = stats[3 * c:3 * c + 3]
        n_blocks = seq // dil // BAND
        per_step = A_TILES_PER_STEP[dil]

        def rows(start, n, dil=dil):
            if dil == 1:
                return _rows(start, n, BAND)
            return pl.ds(start, n, stride=dil)

        def tiles(q_starts, back, n_keys, bias_cols, c=c, rows=rows, so=so, sm=sm, sl=sl):
            both = range(2)
            q_rows = [rows(qs, BAND) for qs in q_starts]
            k_rows = [rows(qs - back, n_keys) for qs in q_starts]
            ks = [kf[kr, :].astype(BF16) for kr in k_rows]
            vs = [vf[kr, :].astype(BF16) for kr in k_rows]
            qms = [[jnp.where(_head_mask((BAND, PAIR), hh), qf[qr, :], 0.0).astype(BF16) for hh in both]
                   for qr in q_rows]
            ss = [[_dot_nt(qms[u][hh], ks[u]) for hh in both] for u in range(len(q_starts))]
            ms, ls, ps = [], [], []
            for u in range(len(q_starts)):
                for hh in both:
                    s = ss[u][hh] + bias_ref[c, pair * 2 + hh, :, bias_cols]
                    m = jnp.max(s, axis=-1, keepdims=True)
                    p = jnp.exp(s - m)
                    ms.append(m)
                    ls.append(jnp.sum(p, axis=-1, keepdims=True))
                    ps.append(p.astype(BF16))
            os = [_dot(ps[2 * u + hh], vs[u]) for u in range(len(q_starts)) for hh in both]
            for u, qr in enumerate(q_rows):
                sm[qr, :] = jnp.where(hm0, ms[2 * u], ms[2 * u + 1])
                sl[qr, :] = jnp.where(hm0, ls[2 * u], ls[2 * u + 1])
                so[qr, :] = jnp.where(hm0, os[2 * u], os[2 * u + 1])

        first_cols = slice(BAND, 2 * BAND)
        all_cols = slice(0, 2 * BAND)
        span = dil * BAND

        def sweep(total, per_step, q_start_of, *args, tiles=tiles):
            if total <= per_step:
                tiles([q_start_of(u) for u in range(total)], *args)
                return

            def step(g, _):
                tiles([q_start_of(g * per_step + u) for u in range(per_step)], *args)
                return 0

            lax.fori_loop(0, total // per_step, step, 0)

        def later_start(idx, dil=dil, span=span):
            r = idx & (dil - 1)
            n = 1 + (idx >> (dil.bit_length() - 1))
            return r + n * span

        sweep(dil, per_step, lambda idx: idx, 0, BAND, first_cols)
        if n_blocks > 1:
            sweep(dil * (n_blocks - 1), per_step, later_start, span, 2 * BAND, all_cols)

    def merge(i, _):
        rows = _rows(i * BAND, BAND, BAND)
        ms = [stats[3 * c + 1][rows, :] for c in range(3)]
        mx = jnp.maximum(jnp.maximum(ms[0], ms[1]), ms[2])
        ws = [jnp.exp(m - mx) for m in ms]
        num = sum(w * stats[3 * c][rows, :] for c, w in enumerate(ws))
        den = sum(w * stats[3 * c + 2][rows, :] for c, w in enumerate(ws))
        o_ref[0, rows, :] = (num / den).astype(o_ref.dtype)
        return 0

    lax.fori_loop(0, seq // BAND, merge, 0)


def _mixer_a(proj, bias_a):
    b, seq, _ = proj.shape
    blk = COL_A // PAIR

    def col(j):
        return pl.BlockSpec((1, seq, PAIR), lambda i, p: (i, 0, blk + 2 * j + p))

    return pl.pallas_call(
        _mixer_a_kernel,
        grid=(b, 2),
        in_specs=[col(0), col(1), col(2),
                  pl.BlockSpec(bias_a.shape, lambda i, p: (0, 0, 0, 0))],
        out_specs=pl.BlockSpec((1, seq, PAIR), lambda i, p: (i, 0, p)),
        out_shape=jax.ShapeDtypeStruct((b, seq, GROUP_W), BF16),
        scratch_shapes=[pltpu.VMEM((seq, PAIR), F32)] * (3 + 3 * len(A_CONFIGS)),
        compiler_params=_cparams("parallel", "parallel"),
        name="mixer_a_dilated",
    )(proj, proj, proj, bias_a)


LOG2E = math.log2(math.e)


def _softplus(z):
    return jnp.log(1.0 + jnp.exp2(jnp.abs(z) * (-LOG2E))) + jnp.maximum(z, 0.0)


def _rev_cumsums(lms, tri):
    his = [lm.astype(BF16) for lm in lms]
    los = [(lm - hi.astype(F32)).astype(BF16) for lm, hi in zip(lms, his)]
    return [_dot(jnp.concatenate([hi, lo], axis=1), tri) for hi, lo in zip(his, los)]


def _mixer_b_kernel(q_ref, k_ref, v_ref, tri_ref, o_ref):
    seq = q_ref.shape[1]
    t = SB_BLOCK
    n_heads = 4
    past = (lax.broadcasted_iota(jnp.int32, (t, t), 1) < lax.broadcasted_iota(jnp.int32, (t, t), 0))
    masks = [_head_mask((t, PAIR), hh) for hh in range(2)]

    def lanes(h):
        return slice((h // 2) * PAIR, (h // 2 + 1) * PAIR)

    k_max = []
    for h in range(n_heads):
        def norm_step(i, best, h=h):
            kk = k_ref[0, _rows(i * t, t, t), lanes(h)].astype(F32)
            sq = jnp.sum(jnp.where(masks[h % 2], kk * kk, 0.0), axis=-1, keepdims=True)
            return jnp.maximum(best, sq)

        sq_max = lax.fori_loop(0, seq // t, norm_step, jnp.zeros((t, 1), F32))
        k_max.append(jnp.sqrt(jnp.max(sq_max, axis=0, keepdims=True)))

    def q_block(i, _):
        q_rows = _rows(i * t, t, t)
        heads = range(n_heads)
        qms = [jnp.where(masks[h % 2], q_ref[0, q_rows, lanes(h)] * SCALE, 0.0).astype(BF16) for h in heads]
        bounds = []
        for h in heads:
            qf = qms[h].astype(F32)
            bounds.append(jnp.sqrt(jnp.sum(qf * qf, axis=-1, keepdims=True)) * k_max[h])
        zs = [_dot_nt(qms[h], k_ref[0, q_rows, lanes(h)]) for h in heads]
        css = _rev_cumsums([jnp.where(past, _softplus(z), 0.0) for z in zs], tri_ref[...])
        aa = [jnp.where(past, jnp.exp(zs[h] - css[h]), 0.0).astype(BF16) for h in heads]
        accs = [_dot(aa[h], v_ref[0, q_rows, lanes(h)]) for h in heads]
        carries = [cs[:, 0:1] for cs in css]

        def walk(step0, rows, watch, accs, carries):
            qs = [qm[rows] for qm in qms]
            bs = [b[rows] for b in bounds]

            def live(carries):
                top = bs[0][watch] - carries[0][watch]
                for c, b in zip(carries[1:], bs[1:]):
                    top = jnp.maximum(top, b[watch] - c[watch])
                return jnp.max(top)

            def cond(state):
                step, top = state[0], state[1]
                return (step < i) & (top > SB_SKIP)

            def k_block(state):
                step = state[0]
                accs, carries = list(state[2:2 + n_heads]), list(state[2 + n_heads:])
                k_rows = _rows((i - 1 - step) * t, t, t)
                zs = [_dot_nt(qs[h], k_ref[0, k_rows, lanes(h)]) for h in heads]
                css = _rev_cumsums([_softplus(z) for z in zs], tri_ref[...])
                aa = [jnp.exp(zs[h] - css[h] - carries[h]).astype(BF16) for h in heads]
                pvs = [_dot(aa[h], v_ref[0, k_rows, lanes(h)]) for h in heads]
                accs = [accs[h] + pvs[h] for h in heads]
                carries = [carries[h] + css[h][:, 0:1] for h in heads]
                return (step + 1, live(carries), *accs, *carries)

            state = lax.while_loop(cond, k_block, (step0, live(carries), *accs, *carries))
            return state[0], list(state[2:2 + n_heads]), list(state[2 + n_heads:])

        upper, lower = slice(0, SB_UPPER), slice(SB_UPPER, t)
        step, accs, carries = walk(jnp.int32(0), slice(0, t), lower, accs, carries)
        _, accs_up, _ = walk(step, upper, upper, [a[upper] for a in accs], [c[upper] for c in carries])
        accs = [jnp.concatenate([accs_up[h], accs[h][lower]], axis=0) for h in heads]
        for p in range(2):
            o_ref[0, q_rows, p * PAIR:(p + 1) * PAIR] = jnp.where(
                masks[0], accs[2 * p], accs[2 * p + 1]).astype(o_ref.dtype)
        return 0

    lax.fori_loop(0, seq // t, q_block, 0)


def _mixer_b(proj, tri):
    b, seq, _ = proj.shape
    blk = COL_B // GROUP_W

    def col(j):
        return pl.BlockSpec((1, seq, GROUP_W), lambda i: (i, 0, blk + j))

    return pl.pallas_call(
        _mixer_b_kernel,
        grid=(b,),
        in_specs=[col(0), col(1), col(2), pl.BlockSpec(tri.shape, lambda i: (0, 0))],
        out_specs=pl.BlockSpec((1, seq, GROUP_W), lambda i: (i, 0, 0)),
        out_shape=jax.ShapeDtypeStruct((b, seq, GROUP_W), BF16),
        compiler_params=_cparams("parallel"),
        name="mixer_b_stickbreak",
    )(proj, proj, proj, tri)


C_TILES_PER_STEP = 5


def _mixer_c_kernel(sink_ref, q_ref, k_ref, v_ref, bias_ref, o_ref):
    seq = q_ref.shape[1]
    n_blocks = seq // BAND
    hm0 = _head_mask((BAND, PAIR), 0)

    for pb in range(2):
        lanes = slice(pb * PAIR, (pb + 1) * PAIR)

        def tiles(q_starts, back, n_keys, bias_cols, pb=pb, lanes=lanes):
            both = range(2)
            q_rows = [_rows(qs, BAND, BAND) for qs in q_starts]
            k_rows = [_rows(qs - back, n_keys, BAND) for qs in q_starts]
            qms = [[jnp.where(_head_mask((BAND, PAIR), hh), q_ref[0, qr, lanes] * SCALE, 0.0).astype(BF16)
                    for hh in both] for qr in q_rows]
            ss = [[_dot_nt(qms[u][hh], k_ref[0, k_rows[u], :]) for hh in both] for u in range(len(q_starts))]
            dens, ps = [], []
            for u in range(len(q_starts)):
                for hh in both:
                    head = hh * 2 + pb
                    sink = sink_ref[head]
                    s = ss[u][hh] + bias_ref[0, head, :, bias_cols]
                    m = jnp.maximum(jnp.max(s, axis=-1, keepdims=True), sink)
                    p = jnp.exp(s - m)
                    dens.append(jnp.sum(p, axis=-1, keepdims=True) + jnp.exp(sink - m))
                    ps.append(p.astype(BF16))
            os = [_dot(ps[2 * u + hh], v_ref[0, k_rows[u], :]) for u in range(len(q_starts)) for hh in both]
            for u, qr in enumerate(q_rows):
                o_ref[0, qr, lanes] = jnp.where(hm0, os[2 * u] / dens[2 * u],
                                                os[2 * u + 1] / dens[2 * u + 1]).astype(o_ref.dtype)

        tiles([0], 0, BAND, slice(BAND, 2 * BAND))

        def later_step(g, _, tiles=tiles):
            tiles([(1 + g * C_TILES_PER_STEP + u) * BAND for u in range(C_TILES_PER_STEP)],
                  BAND, 2 * BAND, slice(0, 2 * BAND))
            return 0

        lax.fori_loop(0, (n_blocks - 1) // C_TILES_PER_STEP, later_step, 0)


def _mixer_c(proj, bias_c, sinks):
    b, seq, _ = proj.shape
    return pl.pallas_call(
        _mixer_c_kernel,
        grid=(b,),
        in_specs=[pl.BlockSpec(memory_space=pltpu.SMEM),
                  pl.BlockSpec((1, seq, GROUP_W), lambda i: (i, 0, COL_CQ // GROUP_W)),
                  pl.BlockSpec((1, seq, PAIR), lambda i: (i, 0, COL_CK // PAIR)),
                  pl.BlockSpec((1, seq, PAIR), lambda i: (i, 0, COL_CV // PAIR)),
                  pl.BlockSpec(bias_c.shape, lambda i: (0, 0, 0, 0))],
        out_specs=pl.BlockSpec((1, seq, GROUP_W), lambda i: (i, 0, 0)),
        out_shape=jax.ShapeDtypeStruct((b, seq, GROUP_W), BF16),
        compiler_params=_cparams("parallel"),
        name="mixer_c_swa_sink",
    )(sinks, proj, proj, proj, bias_c)


def _mixer_d_kernel(q_ref, k_ref, v_ref, bias_ref, o_ref, vt_scr, sel_scr):
    seq = q_ref.shape[1]
    t = MOBA_BLOCK
    n_blk = seq // t
    n_heads = 4
    feat = lax.broadcasted_iota(jnp.int32, (PAIR, t), 0)
    fmasks = [feat < HEAD_DIM, feat >= HEAD_DIM]
    blk_id = lax.broadcasted_iota(jnp.int32, (n_blk, t), 0)

    def lanes(h):
        return slice((h // 2) * PAIR, (h // 2 + 1) * PAIR)

    km_hi, km_lo = [], []
    for p in range(2):
        kmean = jnp.concatenate(
            [jnp.sum(k_ref[0, j * t:(j + 1) * t, p * PAIR:(p + 1) * PAIR].astype(F32), axis=0, keepdims=True)
             for j in range(n_blk)], axis=0) * (1.0 / t)
        hi = kmean.astype(BF16)
        km_hi.append(hi)
        km_lo.append((kmean - hi.astype(F32)).astype(BF16))
        for j in range(n_blk):
            vt_scr[p, j, 0:PAIR, :] = v_ref[0, j * t:(j + 1) * t, p * PAIR:(p + 1) * PAIR].astype(F32).T.astype(BF16)
            vt_scr[p, j, PAIR:, :] = jnp.ones((MOBA_SUM_ROWS, t), BF16)

    def q_block(i, _):
        q_rows = _rows(i * t, t, t)
        heads = range(n_heads)
        qms = [jnp.where(fmasks[h % 2], q_ref[0, q_rows, lanes(h)].astype(F32).T, 0.0).astype(BF16)
               for h in heads]
        qts = [qm * SCALE for qm in qms]
        gates = [_dot(km_hi[h // 2], qms[h]) + _dot(km_lo[h // 2], qms[h]) for h in heads]
        for h in heads:
            gate = jnp.where(blk_id < i, gates[h], NEG)
            rank = jnp.zeros((n_blk, t), jnp.int32)
            for mth in range(n_blk):
                gm = gate[mth:mth + 1, :]
                ahead = (gm > gate) | ((gm == gate) & (mth < blk_id))
                rank = rank + ahead.astype(jnp.int32)
            chosen = ((rank < MOBA_TOPK) & (blk_id < i)) | (blk_id == i)
            sel_scr[h] = jnp.where(chosen, 0.0, NEG)
        state = []
        for h in heads:
            state += [jnp.full((1, t), NEG, F32), jnp.zeros((PAIR + MOBA_SUM_ROWS, t), F32)]

        def k_blocks(js, state):
            k_rows = [_rows(j * t, t, t) for j in js]
            ss = [[_dot(k_ref[0, kr, lanes(h)], qts[h]) for kr in k_rows] for h in heads]
            ms, alphas, ps = [], [], []
            for h in heads:
                m = state[2 * h]
                s = [ss[h][b] + bias_ref[i - j, h] + sel_scr[h, pl.ds(j, 1), :] for b, j in enumerate(js)]
                m_new = m
                for sb in s:
                    m_new = jnp.maximum(m_new, jnp.max(sb, axis=0, keepdims=True))
                ms.append(m_new)
                alphas.append(jnp.exp(m - m_new))
                ps.append(jnp.concatenate([jnp.exp((sb - m_new).astype(BF16)) for sb in s], axis=0))
            pvs = [_dot(jnp.concatenate([vt_scr[h // 2, j] for j in js], axis=1), ps[h]) for h in heads]
            new_state = []
            for h in heads:
                new_state += [ms[h], state[2 * h + 1] * alphas[h] + pvs[h]]
            return tuple(new_state)

        def block_at(k):
            return jnp.where(k == 0, i, k - 1)

        odd = (i + 1) & 1
        state = lax.fori_loop(0, odd, lambda _, st: k_blocks([i], st), tuple(state))
        state = lax.fori_loop(0, (i + 1) >> 1,
                              lambda g, st: k_blocks([block_at(odd + 2 * g), block_at(odd + 2 * g + 1)], st), state)
        outs = [state[2 * h + 1][:PAIR] / state[2 * h + 1][PAIR:PAIR + 1] for h in heads]
        for p in range(2):
            o_ref[0, q_rows, p * PAIR:(p + 1) * PAIR] = jnp.where(
                fmasks[0], outs[2 * p], outs[2 * p + 1]).T.astype(o_ref.dtype)
        return 0

    lax.fori_loop(0, n_blk, q_block, 0)


def _mixer_d(proj, bias_d):
    b, seq, _ = proj.shape
    blk = COL_D // GROUP_W
    n_blk = seq // MOBA_BLOCK

    def col(j):
        return pl.BlockSpec((1, seq, GROUP_W), lambda i: (i, 0, blk + j))

    return pl.pallas_call(
        _mixer_d_kernel,
        grid=(b,),
        in_specs=[col(0), col(1), col(2),
                  pl.BlockSpec(bias_d.shape, lambda i: (0, 0, 0, 0))],
        out_specs=pl.BlockSpec((1, seq, GROUP_W), lambda i: (i, 0, 0)),
        out_shape=jax.ShapeDtypeStruct((b, seq, GROUP_W), BF16),
        scratch_shapes=[pltpu.VMEM((2, n_blk, PAIR + MOBA_SUM_ROWS, MOBA_BLOCK), BF16),
                        pltpu.VMEM((4, n_blk, MOBA_BLOCK), F32)],
        compiler_params=_cparams("parallel"),
        name="mixer_d_moba",
    )(proj, proj, proj, bias_d)


def _mix_out_rows(x, ys, g_ref, w_ref):
    acc = x
    for gi, y in enumerate(ys):
        cols = slice(gi * GROUP_W, (gi + 1) * GROUP_W)
        yn = _rms(y.astype(F32), g_ref[:, cols]).astype(BF16)
        acc = acc + _dot(yn, w_ref[cols, :])
    return acc


def _cross_rows(x, g_ref, wq_ref, kv_ref, wo_ref):
    h = _rms(x, g_ref[...]).astype(BF16)
    q = (_dot(h, wq_ref[...]) * SCALE).astype(BF16)
    tm = x.shape[0]
    heads = range(4)

    def lanes(h, lo=0):
        return slice(lo + (h // 2) * PAIR, lo + (h // 2 + 1) * PAIR)

    qms = [jnp.where(_head_mask((tm, PAIR), h % 2), q[:, lanes(h)], 0.0).astype(BF16) for h in heads]
    ss = [_dot_nt(qms[h], kv_ref[:, lanes(h)]) for h in heads]
    ls, ps = [], []
    for h in heads:
        m = jnp.max(ss[h], axis=-1, keepdims=True)
        p = jnp.exp(ss[h] - m)
        ls.append(jnp.sum(p, axis=-1, keepdims=True))
        ps.append(p.astype(BF16))
    pvs = [_dot(ps[h], kv_ref[:, lanes(h, X_W)]) for h in heads]
    os = [jnp.where(_head_mask((tm, PAIR), 0), pvs[2 * p] / ls[2 * p], pvs[2 * p + 1] / ls[2 * p + 1]).astype(BF16)
          for p in range(2)]
    return x + _dot(os[0], wo_ref[0:PAIR, :]) + _dot(os[1], wo_ref[PAIR:2 * PAIR, :])


POST_CHUNK = 512


def _post_kernel(x_ref, ya_ref, yb_ref, yc_ref, yd_ref, gg_ref, wout_ref, gc_ref, wq_ref, kv_ref, wo_ref,
                 gm_ref, wu_ref, wd_ref, gf_ref, o_ref, xn_scr, acc_scr, *, final_norm):
    j = pl.program_id(1)

    @pl.when(j == 0)
    def _():
        def chunk(c, _):
            rows = _rows(c * POST_CHUNK, POST_CHUNK, POST_CHUNK)
            ys = [y_ref[rows, :] for y_ref in (ya_ref, yb_ref, yc_ref, yd_ref)]
            acc_scr[rows, :] = _mix_out_rows(x_ref[rows, :], ys, gg_ref, wout_ref)
            x = _cross_rows(acc_scr[rows, :], gc_ref, wq_ref, kv_ref, wo_ref)
            xn_scr[rows, :] = _rms(x, gm_ref[...]).astype(BF16)
            acc_scr[rows, :] = x
            return 0

        lax.fori_loop(0, x_ref.shape[0] // POST_CHUNK, chunk, 0)

    h = _dot(xn_scr[...], wu_ref[...])
    h = jnp.square(jnp.maximum(h, 0.0)).astype(BF16)
    acc_scr[...] += _dot(h, wd_ref[...])

    @pl.when(j == pl.num_programs(1) - 1)
    def _():
        y = acc_scr[...]
        if final_norm:
            y = _rms(y, gf_ref[...])
        o_ref[...] = y


def _post(x, ys, g_group, w_out, g_cross, wq, kv, wo, g_mlp, wu, wd, g_final, final_norm, seq, tm, tf):
    t, d = x.shape
    ff = wu.shape[1]
    mem_len = kv.shape[1]
    tiles_per_seq = seq // tm

    def const(shape):
        return pl.BlockSpec(shape, lambda i, j: (0,) * len(shape))

    y_spec = pl.BlockSpec((tm, GROUP_W), lambda i, j: (i, 0))
    return pl.pallas_call(
        functools.partial(_post_kernel, final_norm=final_norm),
        grid=(t // tm, ff // tf),
        in_specs=[pl.BlockSpec((tm, d), lambda i, j: (i, 0)), y_spec, y_spec, y_spec, y_spec,
                  const((1, 4 * GROUP_W)), const((4 * GROUP_W, d)),
                  const((1, d)), const((d, X_W)),
                  pl.BlockSpec((None, mem_len, 2 * X_W), lambda i, j: (i // tiles_per_seq, 0, 0)),
                  const((X_W, d)),
                  const((1, d)),
                  pl.BlockSpec((d, tf), lambda i, j: (0, j)),
                  pl.BlockSpec((tf, d), lambda i, j: (j, 0)),
                  const((1, d))],
        out_specs=pl.BlockSpec((tm, d), lambda i, j: (i, 0)),
        out_shape=jax.ShapeDtypeStruct((t, d), F32),
        scratch_shapes=[pltpu.VMEM((tm, d), BF16), pltpu.VMEM((tm, d), F32)],
        compiler_params=pltpu.CompilerParams(dimension_semantics=("parallel", "arbitrary"),
                                             vmem_limit_bytes=POST_VMEM_LIMIT),
        name="post_mixer_block",
    )(x, *ys, g_group.reshape(1, -1), w_out, g_cross.reshape(1, d), wq, kv, wo,
      g_mlp.reshape(1, d), wu, wd, g_final.reshape(1, d))


def _permute_c_heads(a, axis, lo):
    idx = np.arange(a.shape[axis])
    seg = np.concatenate([lo + h * HEAD_DIM + np.arange(HEAD_DIM) for h in C_HEAD_ORDER])
    idx[lo:lo + GROUP_W] = seg
    return jnp.take(a, jnp.asarray(idx), axis=axis)


def kernel(x, mem, rel_table, g_mix, w_in, g_group, sinks, w_out, g_cross, g_mem, w_xq, w_xkv, w_xo,
           g_mlp, w_up, w_down, g_final):
    b, seq, d = x.shape
    mem_len = mem.shape[1]
    depth = w_in.shape[0]
    tokens = b * seq

    bkt_a, bkt_c, bkt_d = _static_buckets()
    tab_t = rel_table.astype(F32).T
    bias_a = _rel_bias(tab_t, jnp.asarray(bkt_a), A_BIAS_LO)
    bias_c = _rel_bias(tab_t, jnp.asarray(bkt_c), C_BIAS_LO)
    bias_d = _rel_bias(tab_t, jnp.asarray(bkt_d), D_BIAS_LO)
    tri = np.tril(np.ones((SB_BLOCK, SB_BLOCK), np.float32))
    tri = jnp.asarray(np.concatenate([tri, tri], axis=0), dtype=BF16)

    c_out_lo = 2 * GROUP_W
    xf = x.reshape(tokens, d)
    mem_f = mem.reshape(b * mem_len, d)
    for l in range(depth):
        w_in_l = _permute_c_heads(w_in[l], 1, COL_CQ).astype(BF16)
        g_group_l = _permute_c_heads(g_group[l], 0, c_out_lo)
        w_out_l = _permute_c_heads(w_out[l], 0, c_out_lo).astype(BF16)

        proj = _norm_matmul(xf, g_mix[l], w_in_l, tm=512).reshape(b, seq, IN_WIDTH)
        ya = _mixer_a(proj, bias_a)
        yb = _mixer_b(proj, tri)
        yc = _mixer_c(proj, bias_c, sinks[l].astype(F32))
        yd = _mixer_d(proj, bias_d)
        ys = [y.reshape(tokens, GROUP_W) for y in (ya, yb, yc, yd)]
        kv = _norm_matmul(mem_f, g_mem[l], w_xkv[l].astype(BF16), tm=512).reshape(b, mem_len, 2 * X_W)
        xf = _post(xf, ys, g_group_l, w_out_l, g_cross[l], w_xq[l].astype(BF16), kv, w_xo[l].astype(BF16),
                   g_mlp[l], w_up[l].astype(BF16), w_down[l].astype(BF16), g_final,
                   final_norm=(l == depth - 1), seq=seq, tm=1024, tf=1024)
    return xf.reshape(b, seq, d)
```

```python
import functools
import math

import numpy as np
import jax
import jax.numpy as jnp
from jax import lax
from jax.experimental import pallas as pl
from jax.experimental.pallas import tpu as pltpu

F32 = jnp.float32
BF16 = jnp.bfloat16

HEAD_DIM = 64
PAIR = 2 * HEAD_DIM
NORM_EPS = 1e-6
NEG = -1e30
SCALE = HEAD_DIM ** -0.5
A_CONFIGS = ((128, 1), (512, 4), (2048, 16))
BAND = 128
C_WINDOW = 128
SB_BLOCK = 256
SB_SKIP = -100.0
SB_UPPER = 176
MOBA_BLOCK = 256
MOBA_TOPK = 3
MOBA_SUM_ROWS = 16
REL_BUCKETS = 32
REL_MAX_DIST = 2048
A_BIAS_LO, C_BIAS_LO, D_BIAS_LO = 0, 4, 8
GROUP_W = 256
X_W = 256
COL_A, COL_B, COL_CQ, COL_CK, COL_CV, COL_D = 0, 768, 1536, 1792, 1920, 2048
IN_WIDTH = 2816
C_HEAD_ORDER = (0, 2, 1, 3)

VMEM_LIMIT = 48 * 1024 * 1024
POST_VMEM_LIMIT = 56 * 1024 * 1024


def _cparams(*sem):
    return pltpu.CompilerParams(dimension_semantics=sem, vmem_limit_bytes=VMEM_LIMIT)


def _dot(a, b):
    return lax.dot_general(a, b, (((1,), (0,)), ((), ())), preferred_element_type=F32)


def _dot_nt(a, b):
    return lax.dot_general(a, b, (((1,), (1,)), ((), ())), preferred_element_type=F32)


def _head_mask(shape, hh):
    lane = lax.broadcasted_iota(jnp.int32, shape, len(shape) - 1)
    return (lane < HEAD_DIM) if hh == 0 else (lane >= HEAD_DIM)


def _rows(start, size, align):
    if isinstance(start, int):
        return pl.ds(start, size)
    return pl.ds(pl.multiple_of(start, align), size)


def _bucket_np(dist):
    max_exact = REL_BUCKETS // 2
    n = np.maximum(dist, 0)
    nf = np.maximum(n, 1).astype(np.float64)
    large = max_exact + (np.log(nf / max_exact) / math.log(REL_MAX_DIST / max_exact)
                         * (REL_BUCKETS - max_exact)).astype(np.int32)
    large = np.minimum(large, REL_BUCKETS - 1)
    return np.where(n < max_exact, n, large).astype(np.int32)


def _static_buckets():
    qi = np.arange(BAND)[:, None]
    ki = np.arange(2 * BAND)[None, :]
    dist = qi + BAND - ki
    bkt_a = []
    for window, dil in A_CONFIGS:
        valid = (dist >= 0) & (dist <= window // dil)
        bkt_a.append(np.where(valid, _bucket_np(dist * dil), -1))
    valid_c = (dist >= 0) & (dist < C_WINDOW)
    bkt_c = np.where(valid_c, _bucket_np(dist), -1)[None]
    qd = np.arange(MOBA_BLOCK)[None, :]
    kd = np.arange(MOBA_BLOCK)[:, None]
    bkt_d = []
    for delta in range(8):
        dd = delta * MOBA_BLOCK + qd - kd
        bkt_d.append(np.where(dd >= 0, _bucket_np(dd), -1))
    return (np.stack(bkt_a).astype(np.int32), bkt_c.astype(np.int32),
            np.stack(bkt_d).astype(np.int32))


def _rel_bias_kernel(tab_ref, bkt_ref, o_ref, *, head_lo):
    h = head_lo + pl.program_id(1)
    b = bkt_ref[0]
    acc = jnp.full(b.shape, NEG, F32)
    for k in range(REL_BUCKETS):
        acc = jnp.where(b == k, tab_ref[h, k], acc)
    o_ref[0, 0] = acc


def _rel_bias(tab_t, buckets, head_lo):
    n, r, c = buckets.shape
    return pl.pallas_call(
        functools.partial(_rel_bias_kernel, head_lo=head_lo),
        grid=(n, 4),
        in_specs=[pl.BlockSpec(memory_space=pltpu.SMEM),
                  pl.BlockSpec((1, r, c), lambda i, h: (i, 0, 0))],
        out_specs=pl.BlockSpec((1, 1, r, c), lambda i, h: (i, h, 0, 0)),
        out_shape=jax.ShapeDtypeStruct((n, 4, r, c), F32),
        compiler_params=_cparams("arbitrary", "arbitrary"),
        name="rel_bias",
    )(tab_t, buckets)


def _rms(x, g):
    ms = jnp.mean(x * x, axis=-1, keepdims=True)
    return x * lax.rsqrt(ms + NORM_EPS) * g


def _norm_matmul_kernel(x_ref, g_ref, w_ref, o_ref):
    xn = _rms(x_ref[...], g_ref[...]).astype(BF16)
    o_ref[...] = _dot(xn, w_ref[...]).astype(o_ref.dtype)


def _norm_matmul(x, g, w, tm):
    t, d = x.shape
    n = w.shape[1]
    return pl.pallas_call(
        _norm_matmul_kernel,
        grid=(t // tm,),
        in_specs=[pl.BlockSpec((tm, d), lambda i: (i, 0)),
                  pl.BlockSpec((1, d), lambda i: (0, 0)),
                  pl.BlockSpec((d, n), lambda i: (0, 0))],
        out_specs=pl.BlockSpec((tm, n), lambda i: (i, 0)),
        out_shape=jax.ShapeDtypeStruct((t, n), BF16),
        compiler_params=_cparams("parallel"),
        name="norm_matmul",
    )(x, g.reshape(1, d), w)


A_TILES_PER_STEP = {1: 15, 4: 12, 16: 8}


def _mixer_a_kernel(q_ref, k_ref, v_ref, bias_ref, o_ref, qf, kf, vf, *stats):
    seq = q_ref.shape[1]
    pair = pl.program_id(1)
    chunk = 256
    hm0 = _head_mask((BAND, PAIR), 0)

    def upcast(i, _):
        rows = _rows(i * chunk, chunk, chunk)
        qf[rows, :] = q_ref[0, rows, :].astype(F32) * SCALE
        kf[rows, :] = k_ref[0, rows, :].astype(F32)
        vf[rows, :] = v_ref[0, rows, :].astype(F32)
        return 0

    lax.fori_loop(0, seq // chunk, upcast, 0)

    for c, (_, dil) in enumerate(A_CONFIGS):
        so, sm, sl = stats[3 * c:3 * c + 3]
        n_blocks = seq // dil // BAND
        per_step = A_TILES_PER_STEP[dil]

        def rows(start, n, dil=dil):
            if dil == 1:
                return _rows(start, n, BAND)
            return pl.ds(start, n, stride=dil)

        def tiles(q_starts, back, n_keys, bias_cols, c=c, rows=rows, so=so, sm=sm, sl=sl):
            both = range(2)
            q_rows = [rows(qs, BAND) for qs in q_starts]
            k_rows = [rows(qs - back, n_keys) for qs in q_starts]
            ks = [kf[kr, :].astype(BF16) for kr in k_rows]
            vs = [vf[kr, :].astype(BF16) for kr in k_rows]
            qms = [[jnp.where(_head_mask((BAND, PAIR), hh), qf[qr, :], 0.0).astype(BF16) for hh in both]
                   for qr in q_rows]
            ss = [[_dot_nt(qms[u][hh], ks[u]) for hh in both] for u in range(len(q_starts))]
            ms, ls, ps = [], [], []
            for u in range(len(q_starts)):
                for hh in both:
                    s = ss[u][hh] + bias_ref[c, pair * 2 + hh, :, bias_cols]
                    m = jnp.max(s, axis=-1, keepdims=True)
                    p = jnp.exp(s - m)
                    ms.append(m)
                    ls.append(jnp.sum(p, axis=-1, keepdims=True))
                    ps.append(p.astype(BF16))
            os = [_dot(ps[2 * u + hh], vs[u]) for u in range(len(q_starts)) for hh in both]
            for u, qr in enumerate(q_rows):
                sm[qr, :] = jnp.where(hm0, ms[2 * u], ms[2 * u + 1])
                sl[qr, :] = jnp.where(hm0, ls[2 * u], ls[2 * u + 1])
                so[qr, :] = jnp.where(hm0, os[2 * u], os[2 * u + 1])

        first_cols = slice(BAND, 2 * BAND)
        all_cols = slice(0, 2 * BAND)
        span = dil * BAND

        def sweep(total, per_step, q_start_of, *args, tiles=tiles):
            if total <= per_step:
                tiles([q_start_of(u) for u in range(total)], *args)
                return

            def step(g, _):
                tiles([q_start_of(g * per_step + u) for u in range(per_step)], *args)
                return 0

            lax.fori_loop(0, total // per_step, step, 0)

        def later_start(idx, dil=dil, span=span):
            r = idx & (dil - 1)
            n = 1 + (idx >> (dil.bit_length() - 1))
            return r + n * span

        sweep(dil, per_step, lambda idx: idx, 0, BAND, first_cols)
        if n_blocks > 1:
            sweep(dil * (n_blocks - 1), per_step, later_start, span, 2 * BAND, all_cols)

    def merge(i, _):
        rows = _rows(i * BAND, BAND, BAND)
        ms = [stats[3 * c + 1][rows, :] for c in range(3)]
        mx = jnp.maximum(jnp.maximum(ms[0], ms[1]), ms[2])
        ws = [jnp.exp(m - mx) for m in ms]
        num = sum(w * stats[3 * c][rows, :] for c, w in enumerate(ws))
        den = sum(w * stats[3 * c + 2][rows, :] for c, w in enumerate(ws))
        o_ref[0, rows, :] = (num / den).astype(o_ref.dtype)
        return 0

    lax.fori_loop(0, seq // BAND, merge, 0)


def _mixer_a(proj, bias_a):
    b, seq, _ = proj.shape
    blk = COL_A // PAIR

    def col(j):
        return pl.BlockSpec((1, seq, PAIR), lambda i, p: (i, 0, blk + 2 * j + p))

    return pl.pallas_call(
        _mixer_a_kernel,
        grid=(b, 2),
        in_specs=[col(0), col(1), col(2),
                  pl.BlockSpec(bias_a.shape, lambda i, p: (0, 0, 0, 0))],
        out_specs=pl.BlockSpec((1, seq, PAIR), lambda i, p: (i, 0, p)),
        out_shape=jax.ShapeDtypeStruct((b, seq, GROUP_W), BF16),
        scratch_shapes=[pltpu.VMEM((seq, PAIR), F32)] * (3 + 3 * len(A_CONFIGS)),
        compiler_params=_cparams("parallel", "parallel"),
        name="mixer_a_dilated",
    )(proj, proj, proj, bias_a)


LOG2E = math.log2(math.e)


def _softplus(z):
    return jnp.log(1.0 + jnp.exp2(jnp.abs(z) * (-LOG2E))) + jnp.maximum(z, 0.0)


def _rev_cumsums(lms, tri):
    his = [lm.astype(BF16) for lm in lms]
    los = [(lm - hi.astype(F32)).astype(BF16) for lm, hi in zip(lms, his)]
    return [_dot(jnp.concatenate([hi, lo], axis=1), tri) for hi, lo in zip(his, los)]


def _mixer_b_kernel(q_ref, k_ref, v_ref, tri_ref, o_ref):
    seq = q_ref.shape[1]
    t = SB_BLOCK
    n_heads = 4
    past = (lax.broadcasted_iota(jnp.int32, (t, t), 1) < lax.broadcasted_iota(jnp.int32, (t, t), 0))
    masks = [_head_mask((t, PAIR), hh) for hh in range(2)]

    def lanes(h):
        return slice((h // 2) * PAIR, (h // 2 + 1) * PAIR)

    k_max = []
    for h in range(n_heads):
        def norm_step(i, best, h=h):
            kk = k_ref[0, _rows(i * t, t, t), lanes(h)].astype(F32)
            sq = jnp.sum(jnp.where(masks[h % 2], kk * kk, 0.0), axis=-1, keepdims=True)
            return jnp.maximum(best, sq)

        sq_max = lax.fori_loop(0, seq // t, norm_step, jnp.zeros((t, 1), F32))
        k_max.append(jnp.sqrt(jnp.max(sq_max, axis=0, keepdims=True)))

    def q_block(i, _):
        q_rows = _rows(i * t, t, t)
        heads = range(n_heads)
        qms = [jnp.where(masks[h % 2], q_ref[0, q_rows, lanes(h)] * SCALE, 0.0).astype(BF16) for h in heads]
        bounds = []
        for h in heads:
            qf = qms[h].astype(F32)
            bounds.append(jnp.sqrt(jnp.sum(qf * qf, axis=-1, keepdims=True)) * k_max[h])
        zs = [_dot_nt(qms[h], k_ref[0, q_rows, lanes(h)]) for h in heads]
        css = _rev_cumsums([jnp.where(past, _softplus(z), 0.0) for z in zs], tri_ref[...])
        aa = [jnp.where(past, jnp.exp(zs[h] - css[h]), 0.0).astype(BF16) for h in heads]
        accs = [_dot(aa[h], v_ref[0, q_rows, lanes(h)]) for h in heads]
        carries = [cs[:, 0:1] for cs in css]

        def walk(step0, rows, watch, accs, carries):
            qs = [qm[rows] for qm in qms]
            bs = [b[rows] for b in bounds]

            def live(carries):
                top = bs[0][watch] - carries[0][watch]
                for c, b in zip(carries[1:], bs[1:]):
                    top = jnp.maximum(top, b[watch] - c[watch])
                return jnp.max(top)

            def cond(state):
                step, top = state[0], state[1]
                return (step < i) & (top > SB_SKIP)

            def k_block(state):
                step = state[0]
                accs, carries = list(state[2:2 + n_heads]), list(state[2 + n_heads:])
                k_rows = _rows((i - 1 - step) * t, t, t)
                zs = [_dot_nt(qs[h], k_ref[0, k_rows, lanes(h)]) for h in heads]
                css = _rev_cumsums([_softplus(z) for z in zs], tri_ref[...])
                aa = [jnp.exp(zs[h] - css[h] - carries[h]).astype(BF16) for h in heads]
                pvs = [_dot(aa[h], v_ref[0, k_rows, lanes(h)]) for h in heads]
                accs = [accs[h] + pvs[h] for h in heads]
                carries = [carries[h] + css[h][:, 0:1] for h in heads]
                return (step + 1, live(carries), *accs, *carries)

            state = lax.while_loop(cond, k_block, (step0, live(carries), *accs, *carries))
            return state[0], list(state[2:2 + n_heads]), list(state[2 + n_heads:])

        upper, lower = slice(0, SB_UPPER), slice(SB_UPPER, t)
        step, accs, carries = walk(jnp.int32(0), slice(0, t), lower, accs, carries)
        _, accs_up, _ = walk(step, upper, upper, [a[upper] for a in accs], [c[upper] for c in carries])
        accs = [jnp.concatenate([accs_up[h], accs[h][lower]], axis=0) for h in heads]
        for p in range(2):
            o_ref[0, q_rows, p * PAIR:(p + 1) * PAIR] = jnp.where(
                masks[0], accs[2 * p], accs[2 * p + 1]).astype(o_ref.dtype)
        return 0

    lax.fori_loop(0, seq // t, q_block, 0)


def _mixer_b(proj, tri):
    b, seq, _ = proj.shape
    blk = COL_B // GROUP_W

    def col(j):
        return pl.BlockSpec((1, seq, GROUP_W), lambda i: (i, 0, blk + j))

    return pl.pallas_call(
        _mixer_b_kernel,
        grid=(b,),
        in_specs=[col(0), col(1), col(2), pl.BlockSpec(tri.shape, lambda i: (0, 0))],
        out_specs=pl.BlockSpec((1, seq, GROUP_W), lambda i: (i, 0, 0)),
        out_shape=jax.ShapeDtypeStruct((b, seq, GROUP_W), BF16),
        compiler_params=_cparams("parallel"),
        name="mixer_b_stickbreak",
    )(proj, proj, proj, tri)


C_TILES_PER_STEP = 5


def _mixer_c_kernel(sink_ref, q_ref, k_ref, v_ref, bias_ref, o_ref):
    seq = q_ref.shape[1]
    n_blocks = seq // BAND
    hm0 = _head_mask((BAND, PAIR), 0)

    for pb in range(2):
        lanes = slice(pb * PAIR, (pb + 1) * PAIR)

        def tiles(q_starts, back, n_keys, bias_cols, pb=pb, lanes=lanes):
            both = range(2)
            q_rows = [_rows(qs, BAND, BAND) for qs in q_starts]
            k_rows = [_rows(qs - back, n_keys, BAND) for qs in q_starts]
            qms = [[jnp.where(_head_mask((BAND, PAIR), hh), q_ref[0, qr, lanes] * SCALE, 0.0).astype(BF16)
                    for hh in both] for qr in q_rows]
            ss = [[_dot_nt(qms[u][hh], k_ref[0, k_rows[u], :]) for hh in both] for u in range(len(q_starts))]
            dens, ps = [], []
            for u in range(len(q_starts)):
                for hh in both:
                    head = hh * 2 + pb
                    sink = sink_ref[head]
                    s = ss[u][hh] + bias_ref[0, head, :, bias_cols]
                    m = jnp.maximum(jnp.max(s, axis=-1, keepdims=True), sink)
                    p = jnp.exp(s - m)
                    dens.append(jnp.sum(p, axis=-1, keepdims=True) + jnp.exp(sink - m))
                    ps.append(p.astype(BF16))
            os = [_dot(ps[2 * u + hh], v_ref[0, k_rows[u], :]) for u in range(len(q_starts)) for hh in both]
            for u, qr in enumerate(q_rows):
                o_ref[0, qr, lanes] = jnp.where(hm0, os[2 * u] / dens[2 * u],
                                                os[2 * u + 1] / dens[2 * u + 1]).astype(o_ref.dtype)

        tiles([0], 0, BAND, slice(BAND, 2 * BAND))

        def later_step(g, _, tiles=tiles):
            tiles([(1 + g * C_TILES_PER_STEP + u) * BAND for u in range(C_TILES_PER_STEP)],
                  BAND, 2 * BAND, slice(0, 2 * BAND))
            return 0

        lax.fori_loop(0, (n_blocks - 1) // C_TILES_PER_STEP, later_step, 0)


def _mixer_c(proj, bias_c, sinks):
    b, seq, _ = proj.shape
    return pl.pallas_call(
        _mixer_c_kernel,
        grid=(b,),
        in_specs=[pl.BlockSpec(memory_space=pltpu.SMEM),
                  pl.BlockSpec((1, seq, GROUP_W), lambda i: (i, 0, COL_CQ // GROUP_W)),
                  pl.BlockSpec((1, seq, PAIR), lambda i: (i, 0, COL_CK // PAIR)),
                  pl.BlockSpec((1, seq, PAIR), lambda i: (i, 0, COL_CV // PAIR)),
                  pl.BlockSpec(bias_c.shape, lambda i: (0, 0, 0, 0))],
        out_specs=pl.BlockSpec((1, seq, GROUP_W), lambda i: (i, 0, 0)),
        out_shape=jax.ShapeDtypeStruct((b, seq, GROUP_W), BF16),
        compiler_params=_cparams("parallel"),
        name="mixer_c_swa_sink",
    )(sinks, proj, proj, proj, bias_c)


def _mixer_d_kernel(q_ref, k_ref, v_ref, bias_ref, o_ref, vt_scr, sel_scr):
    seq = q_ref.shape[1]
    t = MOBA_BLOCK
    n_blk = seq // t
    n_heads = 4
    feat = lax.broadcasted_iota(jnp.int32, (PAIR, t), 0)
    fmasks = [feat < HEAD_DIM, feat >= HEAD_DIM]
    blk_id = lax.broadcasted_iota(jnp.int32, (n_blk, t), 0)

    def lanes(h):
        return slice((h // 2) * PAIR, (h // 2 + 1) * PAIR)

    km_hi, km_lo = [], []
    for p in range(2):
        kmean = jnp.concatenate(
            [jnp.sum(k_ref[0, j * t:(j + 1) * t, p * PAIR:(p + 1) * PAIR].astype(F32), axis=0, keepdims=True)
             for j in range(n_blk)], axis=0) * (1.0 / t)
        hi = kmean.astype(BF16)
        km_hi.append(hi)
        km_lo.append((kmean - hi.astype(F32)).astype(BF16))
        for j in range(n_blk):
            vt_scr[p, j, 0:PAIR, :] = v_ref[0, j * t:(j + 1) * t, p * PAIR:(p + 1) * PAIR].astype(F32).T.astype(BF16)
            vt_scr[p, j, PAIR:, :] = jnp.ones((MOBA_SUM_ROWS, t), BF16)

    def q_block(i, _):
        q_rows = _rows(i * t, t, t)
        heads = range(n_heads)
        qms = [jnp.where(fmasks[h % 2], q_ref[0, q_rows, lanes(h)].astype(F32).T, 0.0).astype(BF16)
               for h in heads]
        qts = [qm * SCALE for qm in qms]
        gates = [_dot(km_hi[h // 2], qms[h]) + _dot(km_lo[h // 2], qms[h]) for h in heads]
        for h in heads:
            gate = jnp.where(blk_id < i, gates[h], NEG)
            rank = jnp.zeros((n_blk, t), jnp.int32)
            for mth in range(n_blk):
                gm = gate[mth:mth + 1, :]
                ahead = (gm > gate) | ((gm == gate) & (mth < blk_id))
                rank = rank + ahead.astype(jnp.int32)
            chosen = ((rank < MOBA_TOPK) & (blk_id < i)) | (blk_id == i)
            sel_scr[h] = jnp.where(chosen, 0.0, NEG)
        state = []
        for h in heads:
            state += [jnp.full((1, t), NEG, F32), jnp.zeros((PAIR + MOBA_SUM_ROWS, t), F32)]

        def k_blocks(js, state):
            k_rows = [_rows(j * t, t, t) for j in js]
            ss = [[_dot(k_ref[0, kr, lanes(h)], qts[h]) for kr in k_rows] for h in heads]
            ms, alphas, ps = [], [], []
            for h in heads:
                m = state[2 * h]
                s = [ss[h][b] + bias_ref[i - j, h] + sel_scr[h, pl.ds(j, 1), :] for b, j in enumerate(js)]
                m_new = m
                for sb in s:
                    m_new = jnp.maximum(m_new, jnp.max(sb, axis=0, keepdims=True))
                ms.append(m_new)
                alphas.append(jnp.exp(m - m_new))
                ps.append(jnp.concatenate([jnp.exp((sb - m_new).astype(BF16)) for sb in s], axis=0))
            pvs = [_dot(jnp.concatenate([vt_scr[h // 2, j] for j in js], axis=1), ps[h]) for h in heads]
            new_state = []
            for h in heads:
                new_state += [ms[h], state[2 * h + 1] * alphas[h] + pvs[h]]
            return tuple(new_state)

        def block_at(k):
            return jnp.where(k == 0, i, k - 1)

        odd = (i + 1) & 1
        state = lax.fori_loop(0, odd, lambda _, st: k_blocks([i], st), tuple(state))
        state = lax.fori_loop(0, (i + 1) >> 1,
                              lambda g, st: k_blocks([block_at(odd + 2 * g), block_at(odd + 2 * g + 1)], st), state)
        outs = [state[2 * h + 1][:PAIR] / state[2 * h + 1][PAIR:PAIR + 1] for h in heads]
        for p in range(2):
            o_ref[0, q_rows, p * PAIR:(p + 1) * PAIR] = jnp.where(
                fmasks[0], outs[2 * p], outs[2 * p + 1]).T.astype(o_ref.dtype)
        return 0

    lax.fori_loop(0, n_blk, q_block, 0)


def _mixer_d(proj, bias_d):
    b, seq, _ = proj.shape
    blk = COL_D // GROUP_W
    n_blk = seq // MOBA_BLOCK

    def col(j):
        return pl.BlockSpec((1, seq, GROUP_W), lambda i: (i, 0, blk + j))

    return pl.pallas_call(
        _mixer_d_kernel,
        grid=(b,),
        in_specs=[col(0), col(1), col(2),
                  pl.BlockSpec(bias_d.shape, lambda i: (0, 0, 0, 0))],
        out_specs=pl.BlockSpec((1, seq, GROUP_W), lambda i: (i, 0, 0)),
        out_shape=jax.ShapeDtypeStruct((b, seq, GROUP_W), BF16),
        scratch_shapes=[pltpu.VMEM((2, n_blk, PAIR + MOBA_SUM_ROWS, MOBA_BLOCK), BF16),
                        pltpu.VMEM((4, n_blk, MOBA_BLOCK), F32)],
        compiler_params=_cparams("parallel"),
        name="mixer_d_moba",
    )(proj, proj, proj, bias_d)


def _mix_out_rows(x, ys, g_ref, w_ref):
    acc = x
    for gi, y in enumerate(ys):
        cols = slice(gi * GROUP_W, (gi + 1) * GROUP_W)
        yn = _rms(y.astype(F32), g_ref[:, cols]).astype(BF16)
        acc = acc + _dot(yn, w_ref[cols, :])
    return acc


def _cross_rows(xs, g_ref, wq_ref, kv_ref, wo_ref):
    tm = xs[0].shape[0]
    heads = range(4)
    work = [(c, h) for c in range(len(xs)) for h in heads]

    def lanes(h, lo=0):
        return slice(lo + (h // 2) * PAIR, lo + (h // 2 + 1) * PAIR)

    hs = [_rms(x, g_ref[...]).astype(BF16) for x in xs]
    qs = [(_dot(h, wq_ref[...]) * SCALE).astype(BF16) for h in hs]
    qms = [jnp.where(_head_mask((tm, PAIR), h % 2), qs[c][:, lanes(h)], 0.0).astype(BF16) for c, h in work]
    ss = [_dot_nt(qms[w], kv_ref[:, lanes(h)]) for w, (c, h) in enumerate(work)]
    ls, ps = [], []
    for s in ss:
        m = jnp.max(s, axis=-1, keepdims=True)
        p = jnp.exp(s - m)
        ls.append(jnp.sum(p, axis=-1, keepdims=True))
        ps.append(p.astype(BF16))
    pvs = [_dot(ps[w], kv_ref[:, lanes(h, X_W)]) for w, (c, h) in enumerate(work)]
    outs = []
    for c, x in enumerate(xs):
        os = [jnp.where(_head_mask((tm, PAIR), 0), pvs[4 * c + 2 * p] / ls[4 * c + 2 * p],
                        pvs[4 * c + 2 * p + 1] / ls[4 * c + 2 * p + 1]).astype(BF16) for p in range(2)]
        outs.append(x + _dot(os[0], wo_ref[0:PAIR, :]) + _dot(os[1], wo_ref[PAIR:2 * PAIR, :]))
    return outs


POST_CHUNK = 512


def _post_kernel(x_ref, ya_ref, yb_ref, yc_ref, yd_ref, gg_ref, wout_ref, gc_ref, wq_ref, kv_ref, wo_ref,
                 gm_ref, wu_ref, wd_ref, gf_ref, o_ref, xn_scr, acc_scr, *, final_norm):
    j = pl.program_id(1)

    @pl.when(j == 0)
    def _():
        chunks = [pl.ds(c * POST_CHUNK, POST_CHUNK) for c in range(x_ref.shape[0] // POST_CHUNK)]
        for rows in chunks:
            ys = [y_ref[rows, :] for y_ref in (ya_ref, yb_ref, yc_ref, yd_ref)]
            acc_scr[rows, :] = _mix_out_rows(x_ref[rows, :], ys, gg_ref, wout_ref)
        xs = _cross_rows([acc_scr[rows, :] for rows in chunks], gc_ref, wq_ref, kv_ref, wo_ref)
        for rows, x in zip(chunks, xs):
            xn_scr[rows, :] = _rms(x, gm_ref[...]).astype(BF16)
            acc_scr[rows, :] = x

    h = _dot(xn_scr[...], wu_ref[...])
    h = jnp.square(jnp.maximum(h, 0.0)).astype(BF16)
    acc_scr[...] += _dot(h, wd_ref[...])

    @pl.when(j == pl.num_programs(1) - 1)
    def _():
        y = acc_scr[...]
        if final_norm:
            y = _rms(y, gf_ref[...])
        o_ref[...] = y


def _post(x, ys, g_group, w_out, g_cross, wq, kv, wo, g_mlp, wu, wd, g_final, final_norm, seq, tm, tf):
    t, d = x.shape
    ff = wu.shape[1]
    mem_len = kv.shape[1]
    tiles_per_seq = seq // tm

    def const(shape):
        return pl.BlockSpec(shape, lambda i, j: (0,) * len(shape))

    y_spec = pl.BlockSpec((tm, GROUP_W), lambda i, j: (i, 0))
    return pl.pallas_call(
        functools.partial(_post_kernel, final_norm=final_norm),
        grid=(t // tm, ff // tf),
        in_specs=[pl.BlockSpec((tm, d), lambda i, j: (i, 0)), y_spec, y_spec, y_spec, y_spec,
                  const((1, 4 * GROUP_W)), const((4 * GROUP_W, d)),
                  const((1, d)), const((d, X_W)),
                  pl.BlockSpec((None, mem_len, 2 * X_W), lambda i, j: (i // tiles_per_seq, 0, 0)),
                  const((X_W, d)),
                  const((1, d)),
                  pl.BlockSpec((d, tf), lambda i, j: (0, j)),
                  pl.BlockSpec((tf, d), lambda i, j: (j, 0)),
                  const((1, d))],
        out_specs=pl.BlockSpec((tm, d), lambda i, j: (i, 0)),
        out_shape=jax.ShapeDtypeStruct((t, d), F32),
        scratch_shapes=[pltpu.VMEM((tm, d), BF16), pltpu.VMEM((tm, d), F32)],
        compiler_params=pltpu.CompilerParams(dimension_semantics=("parallel", "arbitrary"),
                                             vmem_limit_bytes=POST_VMEM_LIMIT),
        name="post_mixer_block",
    )(x, *ys, g_group.reshape(1, -1), w_out, g_cross.reshape(1, d), wq, kv, wo,
      g_mlp.reshape(1, d), wu, wd, g_final.reshape(1, d))


def _permute_c_heads(a, axis, lo):
    parts = [lax.slice_in_dim(a, 0, lo, axis=axis)]
    parts += [lax.slice_in_dim(a, lo + h * HEAD_DIM, lo + (h + 1) * HEAD_DIM, axis=axis) for h in C_HEAD_ORDER]
    parts.append(lax.slice_in_dim(a, lo + GROUP_W, a.shape[axis], axis=axis))
    return jnp.concatenate(parts, axis=axis)


def kernel(x, mem, rel_table, g_mix, w_in, g_group, sinks, w_out, g_cross, g_mem, w_xq, w_xkv, w_xo,
           g_mlp, w_up, w_down, g_final):
    b, seq, d = x.shape
    mem_len = mem.shape[1]
    depth = w_in.shape[0]
    tokens = b * seq

    bkt_a, bkt_c, bkt_d = _static_buckets()
    tab_t = rel_table.astype(F32).T
    bias_a = _rel_bias(tab_t, jnp.asarray(bkt_a), A_BIAS_LO)
    bias_c = _rel_bias(tab_t, jnp.asarray(bkt_c), C_BIAS_LO)
    bias_d = _rel_bias(tab_t, jnp.asarray(bkt_d), D_BIAS_LO)
    tri = np.tril(np.ones((SB_BLOCK, SB_BLOCK), np.float32))
    tri = jnp.asarray(np.concatenate([tri, tri], axis=0), dtype=BF16)

    c_out_lo = 2 * GROUP_W
    xf = x.reshape(tokens, d)
    mem_f = mem.reshape(b * mem_len, d)
    for l in range(depth):
        w_in_l = _permute_c_heads(w_in[l], 1, COL_CQ).astype(BF16)
        g_group_l = _permute_c_heads(g_group[l], 0, c_out_lo)
        w_out_l = _permute_c_heads(w_out[l], 0, c_out_lo).astype(BF16)

        proj = _norm_matmul(xf, g_mix[l], w_in_l, tm=512).reshape(b, seq, IN_WIDTH)
        ya = _mixer_a(proj, bias_a)
        yb = _mixer_b(proj, tri)
        yc = _mixer_c(proj, bias_c, sinks[l].astype(F32))
        yd = _mixer_d(proj, bias_d)
        ys = [y.reshape(tokens, GROUP_W) for y in (ya, yb, yc, yd)]
        kv = _norm_matmul(mem_f, g_mem[l], w_xkv[l].astype(BF16), tm=512).reshape(b, mem_len, 2 * X_W)
        xf = _post(xf, ys, g_group_l, w_out_l, g_cross[l], w_xq[l].astype(BF16), kv, w_xo[l].astype(BF16),
                   g_mlp[l], w_up[l].astype(BF16), w_down[l].astype(BF16), g_final,
                   final_norm=(l == depth - 1), seq=seq, tm=1024, tf=1024)
    return xf.reshape(b, seq, d)
```

```python
import functools
import math

import numpy as np
import jax
import jax.numpy as jnp
from jax import lax
from jax.experimental import pallas as pl
from jax.experimental.pallas import tpu as pltpu

F32 = jnp.float32
BF16 = jnp.bfloat16

HEAD_DIM = 64
PAIR = 2 * HEAD_DIM
NORM_EPS = 1e-6
NEG = -1e30
SCALE = HEAD_DIM ** -0.5
A_CONFIGS = ((128, 1), (512, 4), (2048, 16))
BAND = 128
C_WINDOW = 128
SB_BLOCK = 256
SB_SKIP = -100.0
SB_UPPER = 176
MOBA_BLOCK = 256
MOBA_TOPK = 3
MOBA_SUM_ROWS = 16
REL_BUCKETS = 32
REL_MAX_DIST = 2048
A_BIAS_LO, C_BIAS_LO, D_BIAS_LO = 0, 4, 8
GROUP_W = 256
X_W = 256
COL_A, COL_B, COL_CQ, COL_CK, COL_CV, COL_D = 0, 768, 1536, 1792, 1920, 2048
IN_WIDTH = 2816
C_HEAD_ORDER = (0, 2, 1, 3)

VMEM_LIMIT = 48 * 1024 * 1024
POST_VMEM_LIMIT = 56 * 1024 * 1024


def _cparams(*sem):
    return pltpu.CompilerParams(dimension_semantics=sem, vmem_limit_bytes=VMEM_LIMIT)


def _dot(a, b):
    return lax.dot_general(a, b, (((1,), (0,)), ((), ())), preferred_element_type=F32)


def _dot_nt(a, b):
    return lax.dot_general(a, b, (((1,), (1,)), ((), ())), preferred_element_type=F32)


def _head_mask(shape, hh):
    lane = lax.broadcasted_iota(jnp.int32, shape, len(shape) - 1)
    return (lane < HEAD_DIM) if hh == 0 else (lane >= HEAD_DIM)


def _rows(start, size, align):
    if isinstance(start, int):
        return pl.ds(start, size)
    return pl.ds(pl.multiple_of(start, align), size)


def _bucket_np(dist):
    max_exact = REL_BUCKETS // 2
    n = np.maximum(dist, 0)
    nf = np.maximum(n, 1).astype(np.float64)
    large = max_exact + (np.log(nf / max_exact) / math.log(REL_MAX_DIST / max_exact)
                         * (REL_BUCKETS - max_exact)).astype(np.int32)
    large = np.minimum(large, REL_BUCKETS - 1)
    return np.where(n < max_exact, n, large).astype(np.int32)


def _static_buckets():
    qi = np.arange(BAND)[:, None]
    ki = np.arange(2 * BAND)[None, :]
    dist = qi + BAND - ki
    bkt_a = []
    for window, dil in A_CONFIGS:
        valid = (dist >= 0) & (dist <= window // dil)
        bkt_a.append(np.where(valid, _bucket_np(dist * dil), -1))
    valid_c = (dist >= 0) & (dist < C_WINDOW)
    bkt_c = np.where(valid_c, _bucket_np(dist), -1)[None]
    qd = np.arange(MOBA_BLOCK)[None, :]
    kd = np.arange(MOBA_BLOCK)[:, None]
    bkt_d = []
    for delta in range(8):
        dd = delta * MOBA_BLOCK + qd - kd
        bkt_d.append(np.where(dd >= 0, _bucket_np(dd), -1))
    return (np.stack(bkt_a).astype(np.int32), bkt_c.astype(np.int32),
            np.stack(bkt_d).astype(np.int32))


def _rel_bias_kernel(tab_ref, bkt_ref, o_ref, *, head_lo):
    h = head_lo + pl.program_id(1)
    b = bkt_ref[0]
    acc = jnp.full(b.shape, NEG, F32)
    for k in range(REL_BUCKETS):
        acc = jnp.where(b == k, tab_ref[h, k], acc)
    o_ref[0, 0] = acc


def _rel_bias(tab_t, buckets, head_lo):
    n, r, c = buckets.shape
    return pl.pallas_call(
        functools.partial(_rel_bias_kernel, head_lo=head_lo),
        grid=(n, 4),
        in_specs=[pl.BlockSpec(memory_space=pltpu.SMEM),
                  pl.BlockSpec((1, r, c), lambda i, h: (i, 0, 0))],
        out_specs=pl.BlockSpec((1, 1, r, c), lambda i, h: (i, h, 0, 0)),
        out_shape=jax.ShapeDtypeStruct((n, 4, r, c), F32),
        compiler_params=_cparams("arbitrary", "arbitrary"),
        name="rel_bias",
    )(tab_t, buckets)


def _rms(x, g):
    ms = jnp.mean(x * x, axis=-1, keepdims=True)
    return x * lax.rsqrt(ms + NORM_EPS) * g


def _norm_matmul_kernel(x_ref, g_ref, w_ref, o_ref):
    xn = _rms(x_ref[...], g_ref[...]).astype(BF16)
    o_ref[...] = _dot(xn, w_ref[...]).astype(o_ref.dtype)


def _norm_matmul(x, g, w, tm):
    t, d = x.shape
    n = w.shape[1]
    return pl.pallas_call(
        _norm_matmul_kernel,
        grid=(t // tm,),
        in_specs=[pl.BlockSpec((tm, d), lambda i: (i, 0)),
                  pl.BlockSpec((1, d), lambda i: (0, 0)),
                  pl.BlockSpec((d, n), lambda i: (0, 0))],
        out_specs=pl.BlockSpec((tm, n), lambda i: (i, 0)),
        out_shape=jax.ShapeDtypeStruct((t, n), BF16),
        compiler_params=_cparams("parallel"),
        name="norm_matmul",
    )(x, g.reshape(1, d), w)


A_TILES_PER_STEP = {1: 15, 4: 12, 16: 8}


def _mixer_a_kernel(q_ref, k_ref, v_ref, bias_ref, o_ref, qf, kf, vf, *stats):
    seq = q_ref.shape[1]
    pair = pl.program_id(1)
    chunk = 256
    hm0 = _head_mask((BAND, PAIR), 0)

    def upcast(i, _):
        rows = _rows(i * chunk, chunk, chunk)
        qf[rows, :] = q_ref[0, rows, :].astype(F32) * SCALE
        kf[rows, :] = k_ref[0, rows, :].astype(F32)
        vf[rows, :] = v_ref[0, rows, :].astype(F32)
        return 0

    lax.fori_loop(0, seq // chunk, upcast, 0)

    for c, (_, dil) in enumerate(A_CONFIGS):
        so, sm, sl = stats[3 * c:3 * c + 3]
        n_blocks = seq // dil // BAND
        per_step = A_TILES_PER_STEP[dil]

        def rows(start, n, dil=dil):
            if dil == 1:
                return _rows(start, n, BAND)
            return pl.ds(start, n, stride=dil)

        def tiles(q_starts, back, n_keys, bias_cols, c=c, rows=rows, so=so, sm=sm, sl=sl):
            both = range(2)
            q_rows = [rows(qs, BAND) for qs in q_starts]
            k_rows = [rows(qs - back, n_keys) for qs in q_starts]
            ks = [kf[kr, :].astype(BF16) for kr in k_rows]
            vs = [vf[kr, :].astype(BF16) for kr in k_rows]
            qms = [[jnp.where(_head_mask((BAND, PAIR), hh), qf[qr, :], 0.0).astype(BF16) for hh in both]
                   for qr in q_rows]
            ss = [[_dot_nt(qms[u][hh], ks[u]) for hh in both] for u in range(len(q_starts))]
            ms, ls, ps = [], [], []
            for u in range(len(q_starts)):
                for hh in both:
                    s = ss[u][hh] + bias_ref[c, pair * 2 + hh, :, bias_cols]
                    m = jnp.max(s, axis=-1, keepdims=True)
                    p = jnp.exp(s - m)
                    ms.append(m)
                    ls.append(jnp.sum(p, axis=-1, keepdims=True))
                    ps.append(p.astype(BF16))
            os = [_dot(ps[2 * u + hh], vs[u]) for u in range(len(q_starts)) for hh in both]
            for u, qr in enumerate(q_rows):
                sm[qr, :] = jnp.where(hm0, ms[2 * u], ms[2 * u + 1])
                sl[qr, :] = jnp.where(hm0, ls[2 * u], ls[2 * u + 1])
                so[qr, :] = jnp.where(hm0, os[2 * u], os[2 * u + 1])

        first_cols = slice(BAND, 2 * BAND)
        all_cols = slice(0, 2 * BAND)
        span = dil * BAND

        def sweep(total, per_step, q_start_of, *args, tiles=tiles):
            if total <= per_step:
                tiles([q_start_of(u) for u in range(total)], *args)
                return

            def step(g, _):
                tiles([q_start_of(g * per_step + u) for u in range(per_step)], *args)
                return 0

            lax.fori_loop(0, total // per_step, step, 0)

        def later_start(idx, dil=dil, span=span):
            r = idx & (dil - 1)
            n = 1 + (idx >> (dil.bit_length() - 1))
            return r + n * span

        sweep(dil, per_step, lambda idx: idx, 0, BAND, first_cols)
        if n_blocks > 1:
            sweep(dil * (n_blocks - 1), per_step, later_start, span, 2 * BAND, all_cols)

    def merge(i, _):
        rows = _rows(i * BAND, BAND, BAND)
        ms = [stats[3 * c + 1][rows, :] for c in range(3)]
        mx = jnp.maximum(jnp.maximum(ms[0], ms[1]), ms[2])
        ws = [jnp.exp(m - mx) for m in ms]
        num = sum(w * stats[3 * c][rows, :] for c, w in enumerate(ws))
        den = sum(w * stats[3 * c + 2][rows, :] for c, w in enumerate(ws))
        o_ref[0, rows, :] = (num / den).astype(o_ref.dtype)
        return 0

    lax.fori_loop(0, seq // BAND, merge, 0)


def _mixer_a(proj, bias_a):
    b, seq, _ = proj.shape
    blk = COL_A // PAIR

    def col(j):
        return pl.BlockSpec((1, seq, PAIR), lambda i, p: (i, 0, blk + 2 * j + p))

    return pl.pallas_call(
        _mixer_a_kernel,
        grid=(b, 2),
        in_specs=[col(0), col(1), col(2),
                  pl.BlockSpec(bias_a.shape, lambda i, p: (0, 0, 0, 0))],
        out_specs=pl.BlockSpec((1, seq, PAIR), lambda i, p: (i, 0, p)),
        out_shape=jax.ShapeDtypeStruct((b, seq, GROUP_W), BF16),
        scratch_shapes=[pltpu.VMEM((seq, PAIR), F32)] * (3 + 3 * len(A_CONFIGS)),
        compiler_params=_cparams("parallel", "parallel"),
        name="mixer_a_dilated",
    )(proj, proj, proj, bias_a)


LOG2E = math.log2(math.e)


def _softplus(z):
    return jnp.log(1.0 + jnp.exp2(jnp.abs(z) * (-LOG2E))) + jnp.maximum(z, 0.0)


def _rev_cumsums(lms, tri):
    his = [lm.astype(BF16) for lm in lms]
    los = [(lm - hi.astype(F32)).astype(BF16) for lm, hi in zip(lms, his)]
    return [_dot(jnp.concatenate([hi, lo], axis=1), tri) for hi, lo in zip(his, los)]


def _mixer_b_kernel(q_ref, k_ref, v_ref, tri_ref, o_ref):
    seq = q_ref.shape[1]
    t = SB_BLOCK
    n_heads = 4
    past = (lax.broadcasted_iota(jnp.int32, (t, t), 1) < lax.broadcasted_iota(jnp.int32, (t, t), 0))
    masks = [_head_mask((t, PAIR), hh) for hh in range(2)]

    def lanes(h):
        return slice((h // 2) * PAIR, (h // 2 + 1) * PAIR)

    k_max = []
    for h in range(n_heads):
        def norm_step(i, best, h=h):
            kk = k_ref[0, _rows(i * t, t, t), lanes(h)].astype(F32)
            sq = jnp.sum(jnp.where(masks[h % 2], kk * kk, 0.0), axis=-1, keepdims=True)
            return jnp.maximum(best, sq)

        sq_max = lax.fori_loop(0, seq // t, norm_step, jnp.zeros((t, 1), F32))
        k_max.append(jnp.sqrt(jnp.max(sq_max, axis=0, keepdims=True)))

    def q_block(i, _):
        q_rows = _rows(i * t, t, t)
        heads = range(n_heads)
        qms = [jnp.where(masks[h % 2], q_ref[0, q_rows, lanes(h)] * SCALE, 0.0).astype(BF16) for h in heads]
        bounds = []
        for h in heads:
            qf = qms[h].astype(F32)
            bounds.append(jnp.sqrt(jnp.sum(qf * qf, axis=-1, keepdims=True)) * k_max[h])
        p_rows = _rows(jnp.maximum(i - 1, 0) * t, t, t)
        no_prev = jnp.where(i == 0, -NEG, 0.0)
        zs = [_dot_nt(qms[h], k_ref[0, q_rows, lanes(h)]) for h in heads]
        zp = [_dot_nt(qms[h], k_ref[0, p_rows, lanes(h)]) for h in heads]
        css = _rev_cumsums([jnp.where(past, _softplus(z), 0.0) for z in zs] + [_softplus(z) for z in zp],
                           tri_ref[...])
        css, csp = css[:n_heads], css[n_heads:]
        carries = [cs[:, 0:1] for cs in css]
        aa = [jnp.where(past, jnp.exp(zs[h] - css[h]), 0.0).astype(BF16) for h in heads]
        ap = [jnp.exp(zp[h] - csp[h] - (carries[h] + no_prev)).astype(BF16) for h in heads]
        accs = [_dot(aa[h], v_ref[0, q_rows, lanes(h)]) for h in heads]
        accp = [_dot(ap[h], v_ref[0, p_rows, lanes(h)]) for h in heads]
        accs = [accs[h] + accp[h] for h in heads]
        carries = [carries[h] + csp[h][:, 0:1] for h in heads]

        def walk(step0, rows, watch, accs, carries):
            qs = [qm[rows] for qm in qms]
            bs = [b[rows] for b in bounds]

            def live(carries):
                top = bs[0][watch] - carries[0][watch]
                for c, b in zip(carries[1:], bs[1:]):
                    top = jnp.maximum(top, b[watch] - c[watch])
                return jnp.max(top)

            def cond(state):
                step, top = state[0], state[1]
                return (step < i) & (top > SB_SKIP)

            def k_block(state):
                step = state[0]
                accs, carries = list(state[2:2 + n_heads]), list(state[2 + n_heads:])
                k_rows = _rows((i - 1 - step) * t, t, t)
                zs = [_dot_nt(qs[h], k_ref[0, k_rows, lanes(h)]) for h in heads]
                css = _rev_cumsums([_softplus(z) for z in zs], tri_ref[...])
                aa = [jnp.exp(zs[h] - css[h] - carries[h]).astype(BF16) for h in heads]
                pvs = [_dot(aa[h], v_ref[0, k_rows, lanes(h)]) for h in heads]
                accs = [accs[h] + pvs[h] for h in heads]
                carries = [carries[h] + css[h][:, 0:1] for h in heads]
                return (step + 1, live(carries), *accs, *carries)

            state = lax.while_loop(cond, k_block, (step0, live(carries), *accs, *carries))
            return state[0], list(state[2:2 + n_heads]), list(state[2 + n_heads:])

        upper, lower = slice(0, SB_UPPER), slice(SB_UPPER, t)
        step, accs, carries = walk(jnp.int32(1), slice(0, t), lower, accs, carries)
        _, accs_up, _ = walk(step, upper, upper, [a[upper] for a in accs], [c[upper] for c in carries])
        accs = [jnp.concatenate([accs_up[h], accs[h][lower]], axis=0) for h in heads]
        for p in range(2):
            o_ref[0, q_rows, p * PAIR:(p + 1) * PAIR] = jnp.where(
                masks[0], accs[2 * p], accs[2 * p + 1]).astype(o_ref.dtype)
        return 0

    lax.fori_loop(0, seq // t, q_block, 0)


def _mixer_b(proj, tri):
    b, seq, _ = proj.shape
    blk = COL_B // GROUP_W

    def col(j):
        return pl.BlockSpec((1, seq, GROUP_W), lambda i: (i, 0, blk + j))

    return pl.pallas_call(
        _mixer_b_kernel,
        grid=(b,),
        in_specs=[col(0), col(1), col(2), pl.BlockSpec(tri.shape, lambda i: (0, 0))],
        out_specs=pl.BlockSpec((1, seq, GROUP_W), lambda i: (i, 0, 0)),
        out_shape=jax.ShapeDtypeStruct((b, seq, GROUP_W), BF16),
        compiler_params=_cparams("parallel"),
        name="mixer_b_stickbreak",
    )(proj, proj, proj, tri)


C_TILES_PER_STEP = 5


def _mixer_c_kernel(sink_ref, q_ref, k_ref, v_ref, bias_ref, o_ref):
    seq = q_ref.shape[1]
    n_blocks = seq // BAND
    hm0 = _head_mask((BAND, PAIR), 0)

    for pb in range(2):
        lanes = slice(pb * PAIR, (pb + 1) * PAIR)

        def tiles(q_starts, back, n_keys, bias_cols, pb=pb, lanes=lanes):
            both = range(2)
            q_rows = [_rows(qs, BAND, BAND) for qs in q_starts]
            k_rows = [_rows(qs - back, n_keys, BAND) for qs in q_starts]
            qms = [[jnp.where(_head_mask((BAND, PAIR), hh), q_ref[0, qr, lanes] * SCALE, 0.0).astype(BF16)
                    for hh in both] for qr in q_rows]
            ss = [[_dot_nt(qms[u][hh], k_ref[0, k_rows[u], :]) for hh in both] for u in range(len(q_starts))]
            dens, ps = [], []
            for u in range(len(q_starts)):
                for hh in both:
                    head = hh * 2 + pb
                    sink = sink_ref[head]
                    s = ss[u][hh] + bias_ref[0, head, :, bias_cols]
                    m = jnp.maximum(jnp.max(s, axis=-1, keepdims=True), sink)
                    p = jnp.exp(s - m)
                    dens.append(jnp.sum(p, axis=-1, keepdims=True) + jnp.exp(sink - m))
                    ps.append(p.astype(BF16))
            os = [_dot(ps[2 * u + hh], v_ref[0, k_rows[u], :]) for u in range(len(q_starts)) for hh in both]
            for u, qr in enumerate(q_rows):
                o_ref[0, qr, lanes] = jnp.where(hm0, os[2 * u] / dens[2 * u],
                                                os[2 * u + 1] / dens[2 * u + 1]).astype(o_ref.dtype)

        tiles([0], 0, BAND, slice(BAND, 2 * BAND))

        def later_step(g, _, tiles=tiles):
            tiles([(1 + g * C_TILES_PER_STEP + u) * BAND for u in range(C_TILES_PER_STEP)],
                  BAND, 2 * BAND, slice(0, 2 * BAND))
            return 0

        lax.fori_loop(0, (n_blocks - 1) // C_TILES_PER_STEP, later_step, 0)


def _mixer_c(proj, bias_c, sinks):
    b, seq, _ = proj.shape
    return pl.pallas_call(
        _mixer_c_kernel,
        grid=(b,),
        in_specs=[pl.BlockSpec(memory_space=pltpu.SMEM),
                  pl.BlockSpec((1, seq, GROUP_W), lambda i: (i, 0, COL_CQ // GROUP_W)),
                  pl.BlockSpec((1, seq, PAIR), lambda i: (i, 0, COL_CK // PAIR)),
                  pl.BlockSpec((1, seq, PAIR), lambda i: (i, 0, COL_CV // PAIR)),
                  pl.BlockSpec(bias_c.shape, lambda i: (0, 0, 0, 0))],
        out_specs=pl.BlockSpec((1, seq, GROUP_W), lambda i: (i, 0, 0)),
        out_shape=jax.ShapeDtypeStruct((b, seq, GROUP_W), BF16),
        compiler_params=_cparams("parallel"),
        name="mixer_c_swa_sink",
    )(sinks, proj, proj, proj, bias_c)


def _mixer_d_kernel(q_ref, k_ref, v_ref, bias_ref, o_ref, vt_scr, sel_scr):
    seq = q_ref.shape[1]
    t = MOBA_BLOCK
    n_blk = seq // t
    n_heads = 4
    feat = lax.broadcasted_iota(jnp.int32, (PAIR, t), 0)
    fmasks = [feat < HEAD_DIM, feat >= HEAD_DIM]
    blk_id = lax.broadcasted_iota(jnp.int32, (n_blk, t), 0)

    def lanes(h):
        return slice((h // 2) * PAIR, (h // 2 + 1) * PAIR)

    km_hi, km_lo = [], []
    for p in range(2):
        kmean = jnp.concatenate(
            [jnp.sum(k_ref[0, j * t:(j + 1) * t, p * PAIR:(p + 1) * PAIR].astype(F32), axis=0, keepdims=True)
             for j in range(n_blk)], axis=0) * (1.0 / t)
        hi = kmean.astype(BF16)
        km_hi.append(hi)
        km_lo.append((kmean - hi.astype(F32)).astype(BF16))
        for j in range(n_blk):
            vt_scr[p, j, 0:PAIR, :] = v_ref[0, j * t:(j + 1) * t, p * PAIR:(p + 1) * PAIR].astype(F32).T.astype(BF16)
            vt_scr[p, j, PAIR:, :] = jnp.ones((MOBA_SUM_ROWS, t), BF16)

    def q_block(i, _):
        q_rows = _rows(i * t, t, t)
        heads = range(n_heads)
        qms = [jnp.where(fmasks[h % 2], q_ref[0, q_rows, lanes(h)].astype(F32).T, 0.0).astype(BF16)
               for h in heads]
        qts = [qm * SCALE for qm in qms]
        gates = [_dot(km_hi[h // 2], qms[h]) + _dot(km_lo[h // 2], qms[h]) for h in heads]
        for h in heads:
            gate = jnp.where(blk_id < i, gates[h], NEG)
            rank = jnp.zeros((n_blk, t), jnp.int32)
            for mth in range(n_blk):
                gm = gate[mth:mth + 1, :]
                ahead = (gm > gate) | ((gm == gate) & (mth < blk_id))
                rank = rank + ahead.astype(jnp.int32)
            chosen = ((rank < MOBA_TOPK) & (blk_id < i)) | (blk_id == i)
            sel_scr[h] = jnp.where(chosen, 0.0, NEG)
        state = []
        for h in heads:
            state += [jnp.full((1, t), NEG, F32), jnp.zeros((PAIR + MOBA_SUM_ROWS, t), F32)]

        def k_blocks(js, state):
            k_rows = [_rows(j * t, t, t) for j in js]
            ss = [[_dot(k_ref[0, kr, lanes(h)], qts[h]) for kr in k_rows] for h in heads]
            ms, alphas, ps = [], [], []
            for h in heads:
                m = state[2 * h]
                s = [ss[h][b] + bias_ref[i - j, h] + sel_scr[h, pl.ds(j, 1), :] for b, j in enumerate(js)]
                m_new = m
                for sb in s:
                    m_new = jnp.maximum(m_new, jnp.max(sb, axis=0, keepdims=True))
                ms.append(m_new)
                alphas.append(jnp.exp(m - m_new))
                ps.append(jnp.concatenate([jnp.exp((sb - m_new).astype(BF16)) for sb in s], axis=0))
            pvs = [_dot(jnp.concatenate([vt_scr[h // 2, j] for j in js], axis=1), ps[h]) for h in heads]
            new_state = []
            for h in heads:
                new_state += [ms[h], state[2 * h + 1] * alphas[h] + pvs[h]]
            return tuple(new_state)

        def block_at(k):
            return jnp.where(k == 0, i, k - 1)

        odd = (i + 1) & 1
        state = lax.fori_loop(0, odd, lambda _, st: k_blocks([i], st), tuple(state))
        state = lax.fori_loop(0, (i + 1) >> 1,
                              lambda g, st: k_blocks([block_at(odd + 2 * g), block_at(odd + 2 * g + 1)], st), state)
        outs = [state[2 * h + 1][:PAIR] / state[2 * h + 1][PAIR:PAIR + 1] for h in heads]
        for p in range(2):
            o_ref[0, q_rows, p * PAIR:(p + 1) * PAIR] = jnp.where(
                fmasks[0], outs[2 * p], outs[2 * p + 1]).T.astype(o_ref.dtype)
        return 0

    lax.fori_loop(0, n_blk, q_block, 0)


def _mixer_d(proj, bias_d):
    b, seq, _ = proj.shape
    blk = COL_D // GROUP_W
    n_blk = seq // MOBA_BLOCK

    def col(j):
        return pl.BlockSpec((1, seq, GROUP_W), lambda i: (i, 0, blk + j))

    return pl.pallas_call(
        _mixer_d_kernel,
        grid=(b,),
        in_specs=[col(0), col(1), col(2),
                  pl.BlockSpec(bias_d.shape, lambda i: (0, 0, 0, 0))],
        out_specs=pl.BlockSpec((1, seq, GROUP_W), lambda i: (i, 0, 0)),
        out_shape=jax.ShapeDtypeStruct((b, seq, GROUP_W), BF16),
        scratch_shapes=[pltpu.VMEM((2, n_blk, PAIR + MOBA_SUM_ROWS, MOBA_BLOCK), BF16),
                        pltpu.VMEM((4, n_blk, MOBA_BLOCK), F32)],
        compiler_params=_cparams("parallel"),
        name="mixer_d_moba",
    )(proj, proj, proj, bias_d)


def _mix_out_rows(x, ys, g_ref, w_ref):
    acc = x
    for gi, y in enumerate(ys):
        cols = slice(gi * GROUP_W, (gi + 1) * GROUP_W)
        yn = _rms(y.astype(F32), g_ref[:, cols]).astype(BF16)
        acc = acc + _dot(yn, w_ref[cols, :])
    return acc


def _cross_rows(xs, g_ref, wq_ref, kv_ref, wo_ref):
    tm = xs[0].shape[0]
    heads = range(4)
    work = [(c, h) for c in range(len(xs)) for h in heads]

    def lanes(h, lo=0):
        return slice(lo + (h // 2) * PAIR, lo + (h // 2 + 1) * PAIR)

    hs = [_rms(x, g_ref[...]).astype(BF16) for x in xs]
    qs = [(_dot(h, wq_ref[...]) * SCALE).astype(BF16) for h in hs]
    qms = [jnp.where(_head_mask((tm, PAIR), h % 2), qs[c][:, lanes(h)], 0.0).astype(BF16) for c, h in work]
    ss = [_dot_nt(qms[w], kv_ref[:, lanes(h)]) for w, (c, h) in enumerate(work)]
    ls, ps = [], []
    for s in ss:
        m = jnp.max(s, axis=-1, keepdims=True)
        p = jnp.exp(s - m)
        ls.append(jnp.sum(p, axis=-1, keepdims=True))
        ps.append(p.astype(BF16))
    pvs = [_dot(ps[w], kv_ref[:, lanes(h, X_W)]) for w, (c, h) in enumerate(work)]
    outs = []
    for c, x in enumerate(xs):
        os = [jnp.where(_head_mask((tm, PAIR), 0), pvs[4 * c + 2 * p] / ls[4 * c + 2 * p],
                        pvs[4 * c + 2 * p + 1] / ls[4 * c + 2 * p + 1]).astype(BF16) for p in range(2)]
        outs.append(x + _dot(os[0], wo_ref[0:PAIR, :]) + _dot(os[1], wo_ref[PAIR:2 * PAIR, :]))
    return outs


POST_CHUNK = 512


def _post_kernel(x_ref, ya_ref, yb_ref, yc_ref, yd_ref, gg_ref, wout_ref, gc_ref, wq_ref, kv_ref, wo_ref,
                 gm_ref, wu_ref, wd_ref, gf_ref, o_ref, xn_scr, acc_scr, *, final_norm):
    j = pl.program_id(1)

    @pl.when(j == 0)
    def _():
        chunks = [pl.ds(c * POST_CHUNK, POST_CHUNK) for c in range(x_ref.shape[0] // POST_CHUNK)]
        for rows in chunks:
            ys = [y_ref[rows, :] for y_ref in (ya_ref, yb_ref, yc_ref, yd_ref)]
            acc_scr[rows, :] = _mix_out_rows(x_ref[rows, :], ys, gg_ref, wout_ref)
        xs = _cross_rows([acc_scr[rows, :] for rows in chunks], gc_ref, wq_ref, kv_ref, wo_ref)
        for rows, x in zip(chunks, xs):
            xn_scr[rows, :] = _rms(x, gm_ref[...]).astype(BF16)
            acc_scr[rows, :] = x

    h = _dot(xn_scr[...], wu_ref[...])
    h = jnp.square(jnp.maximum(h, 0.0)).astype(BF16)
    acc_scr[...] += _dot(h, wd_ref[...])

    @pl.when(j == pl.num_programs(1) - 1)
    def _():
        y = acc_scr[...]
        if final_norm:
            y = _rms(y, gf_ref[...])
        o_ref[...] = y


def _post(x, ys, g_group, w_out, g_cross, wq, kv, wo, g_mlp, wu, wd, g_final, final_norm, seq, tm, tf):
    t, d = x.shape
    ff = wu.shape[1]
    mem_len = kv.shape[1]
    tiles_per_seq = seq // tm

    def const(shape):
        return pl.BlockSpec(shape, lambda i, j: (0,) * len(shape))

    y_spec = pl.BlockSpec((tm, GROUP_W), lambda i, j: (i, 0))
    return pl.pallas_call(
        functools.partial(_post_kernel, final_norm=final_norm),
        grid=(t // tm, ff // tf),
        in_specs=[pl.BlockSpec((tm, d), lambda i, j: (i, 0)), y_spec, y_spec, y_spec, y_spec,
                  const((1, 4 * GROUP_W)), const((4 * GROUP_W, d)),
                  const((1, d)), const((d, X_W)),
                  pl.BlockSpec((None, mem_len, 2 * X_W), lambda i, j: (i // tiles_per_seq, 0, 0)),
                  const((X_W, d)),
                  const((1, d)),
                  pl.BlockSpec((d, tf), lambda i, j: (0, j)),
                  pl.BlockSpec((tf, d), lambda i, j: (j, 0)),
                  const((1, d))],
        out_specs=pl.BlockSpec((tm, d), lambda i, j: (i, 0)),
        out_shape=jax.ShapeDtypeStruct((t, d), F32),
        scratch_shapes=[pltpu.VMEM((tm, d), BF16), pltpu.VMEM((tm, d), F32)],
        compiler_params=pltpu.CompilerParams(dimension_semantics=("parallel", "arbitrary"),
                                             vmem_limit_bytes=POST_VMEM_LIMIT),
        name="post_mixer_block",
    )(x, *ys, g_group.reshape(1, -1), w_out, g_cross.reshape(1, d), wq, kv, wo,
      g_mlp.reshape(1, d), wu, wd, g_final.reshape(1, d))


def _permute_c_heads(a, axis, lo):
    parts = [lax.slice_in_dim(a, 0, lo, axis=axis)]
    parts += [lax.slice_in_dim(a, lo + h * HEAD_DIM, lo + (h + 1) * HEAD_DIM, axis=axis) for h in C_HEAD_ORDER]
    parts.append(lax.slice_in_dim(a, lo + GROUP_W, a.shape[axis], axis=axis))
    return jnp.concatenate(parts, axis=axis)


def kernel(x, mem, rel_table, g_mix, w_in, g_group, sinks, w_out, g_cross, g_mem, w_xq, w_xkv, w_xo,
           g_mlp, w_up, w_down, g_final):
    b, seq, d = x.shape
    mem_len = mem.shape[1]
    depth = w_in.shape[0]
    tokens = b * seq

    bkt_a, bkt_c, bkt_d = _static_buckets()
    tab_t = rel_table.astype(F32).T
    bias_a = _rel_bias(tab_t, jnp.asarray(bkt_a), A_BIAS_LO)
    bias_c = _rel_bias(tab_t, jnp.asarray(bkt_c), C_BIAS_LO)
    bias_d = _rel_bias(tab_t, jnp.asarray(bkt_d), D_BIAS_LO)
    tri = np.tril(np.ones((SB_BLOCK, SB_BLOCK), np.float32))
    tri = jnp.asarray(np.concatenate([tri, tri], axis=0), dtype=BF16)

    c_out_lo = 2 * GROUP_W
    xf = x.reshape(tokens, d)
    mem_f = mem.reshape(b * mem_len, d)
    for l in range(depth):
        w_in_l = _permute_c_heads(w_in[l], 1, COL_CQ).astype(BF16)
        g_group_l = _permute_c_heads(g_group[l], 0, c_out_lo)
        w_out_l = _permute_c_heads(w_out[l], 0, c_out_lo).astype(BF16)

        proj = _norm_matmul(xf, g_mix[l], w_in_l, tm=512).reshape(b, seq, IN_WIDTH)
        ya = _mixer_a(proj, bias_a)
        yb = _mixer_b(proj, tri)
        yc = _mixer_c(proj, bias_c, sinks[l].astype(F32))
        yd = _mixer_d(proj, bias_d)
        ys = [y.reshape(tokens, GROUP_W) for y in (ya, yb, yc, yd)]
        kv = _norm_matmul(mem_f, g_mem[l], w_xkv[l].astype(BF16), tm=512).reshape(b, mem_len, 2 * X_W)
        xf = _post(xf, ys, g_group_l, w_out_l, g_cross[l], w_xq[l].astype(BF16), kv, w_xo[l].astype(BF16),
                   g_mlp[l], w_up[l].astype(BF16), w_down[l].astype(BF16), g_final,
                   final_norm=(l == depth - 1), seq=seq, tm=1024, tf=1024)
    return xf.reshape(b, seq, d)
```

```python
import functools
import math

import numpy as np
import jax
import jax.numpy as jnp
from jax import lax
from jax.experimental import pallas as pl
from jax.experimental.pallas import tpu as pltpu

F32 = jnp.float32
BF16 = jnp.bfloat16

HEAD_DIM = 64
PAIR = 2 * HEAD_DIM
NORM_EPS = 1e-6
NEG = -1e30
SCALE = HEAD_DIM ** -0.5
A_CONFIGS = ((128, 1), (512, 4), (2048, 16))
BAND = 128
C_WINDOW = 128
SB_BLOCK = 256
SB_SKIP = -100.0
SB_UPPER = 176
MOBA_BLOCK = 256
MOBA_TOPK = 3
MOBA_SUM_ROWS = 16
REL_BUCKETS = 32
REL_MAX_DIST = 2048
A_BIAS_LO, C_BIAS_LO, D_BIAS_LO = 0, 4, 8
GROUP_W = 256
X_W = 256
COL_A, COL_B, COL_CQ, COL_CK, COL_CV, COL_D = 0, 768, 1536, 1792, 1920, 2048
IN_WIDTH = 2816
C_HEAD_ORDER = (0, 2, 1, 3)

VMEM_LIMIT = 48 * 1024 * 1024
POST_VMEM_LIMIT = 56 * 1024 * 1024


def _cparams(*sem):
    return pltpu.CompilerParams(dimension_semantics=sem, vmem_limit_bytes=VMEM_LIMIT)


def _dot(a, b):
    return lax.dot_general(a, b, (((1,), (0,)), ((), ())), preferred_element_type=F32)


def _dot_nt(a, b):
    return lax.dot_general(a, b, (((1,), (1,)), ((), ())), preferred_element_type=F32)


def _head_mask(shape, hh):
    lane = lax.broadcasted_iota(jnp.int32, shape, len(shape) - 1)
    return (lane < HEAD_DIM) if hh == 0 else (lane >= HEAD_DIM)


def _rows(start, size, align):
    if isinstance(start, int):
        return pl.ds(start, size)
    return pl.ds(pl.multiple_of(start, align), size)


def _bucket_np(dist):
    max_exact = REL_BUCKETS // 2
    n = np.maximum(dist, 0)
    nf = np.maximum(n, 1).astype(np.float64)
    large = max_exact + (np.log(nf / max_exact) / math.log(REL_MAX_DIST / max_exact)
                         * (REL_BUCKETS - max_exact)).astype(np.int32)
    large = np.minimum(large, REL_BUCKETS - 1)
    return np.where(n < max_exact, n, large).astype(np.int32)


def _static_buckets():
    qi = np.arange(BAND)[:, None]
    ki = np.arange(2 * BAND)[None, :]
    dist = qi + BAND - ki
    bkt_a = []
    for window, dil in A_CONFIGS:
        valid = (dist >= 0) & (dist <= window // dil)
        bkt_a.append(np.where(valid, _bucket_np(dist * dil), -1))
    valid_c = (dist >= 0) & (dist < C_WINDOW)
    bkt_c = np.where(valid_c, _bucket_np(dist), -1)[None]
    qd = np.arange(MOBA_BLOCK)[None, :]
    kd = np.arange(MOBA_BLOCK)[:, None]
    bkt_d = []
    for delta in range(8):
        dd = delta * MOBA_BLOCK + qd - kd
        bkt_d.append(np.where(dd >= 0, _bucket_np(dd), -1))
    return (np.stack(bkt_a).astype(np.int32), bkt_c.astype(np.int32),
            np.stack(bkt_d).astype(np.int32))


def _rel_bias_kernel(tab_ref, bkt_ref, o_ref, *, head_lo):
    h = head_lo + pl.program_id(1)
    b = bkt_ref[0]
    acc = jnp.full(b.shape, NEG, F32)
    for k in range(REL_BUCKETS):
        acc = jnp.where(b == k, tab_ref[h, k], acc)
    o_ref[0, 0] = acc


def _rel_bias(tab_t, buckets, head_lo):
    n, r, c = buckets.shape
    return pl.pallas_call(
        functools.partial(_rel_bias_kernel, head_lo=head_lo),
        grid=(n, 4),
        in_specs=[pl.BlockSpec(memory_space=pltpu.SMEM),
                  pl.BlockSpec((1, r, c), lambda i, h: (i, 0, 0))],
        out_specs=pl.BlockSpec((1, 1, r, c), lambda i, h: (i, h, 0, 0)),
        out_shape=jax.ShapeDtypeStruct((n, 4, r, c), F32),
        compiler_params=_cparams("arbitrary", "arbitrary"),
        name="rel_bias",
    )(tab_t, buckets)


def _rms(x, g):
    ms = jnp.mean(x * x, axis=-1, keepdims=True)
    return x * lax.rsqrt(ms + NORM_EPS) * g


def _norm_matmul_kernel(x_ref, g_ref, w_ref, o_ref):
    xn = _rms(x_ref[...], g_ref[...]).astype(BF16)
    o_ref[...] = _dot(xn, w_ref[...]).astype(o_ref.dtype)


def _norm_matmul(x, g, w, tm):
    t, d = x.shape
    n = w.shape[1]
    return pl.pallas_call(
        _norm_matmul_kernel,
        grid=(t // tm,),
        in_specs=[pl.BlockSpec((tm, d), lambda i: (i, 0)),
                  pl.BlockSpec((1, d), lambda i: (0, 0)),
                  pl.BlockSpec((d, n), lambda i: (0, 0))],
        out_specs=pl.BlockSpec((tm, n), lambda i: (i, 0)),
        out_shape=jax.ShapeDtypeStruct((t, n), BF16),
        compiler_params=_cparams("parallel"),
        name="norm_matmul",
    )(x, g.reshape(1, d), w)


A_TILES_PER_STEP = {1: 15, 4: 12, 16: 8}


def _mixer_a_kernel(q_ref, k_ref, v_ref, bias_ref, o_ref, qf, kf, vf, *stats):
    seq = q_ref.shape[1]
    pair = pl.program_id(1)
    chunk = 256
    hm0 = _head_mask((BAND, PAIR), 0)

    def upcast(i, _):
        rows = _rows(i * chunk, chunk, chunk)
        qf[rows, :] = q_ref[0, rows, :].astype(F32) * SCALE
        kf[rows, :] = k_ref[0, rows, :].astype(F32)
        vf[rows, :] = v_ref[0, rows, :].astype(F32)
        return 0

    lax.fori_loop(0, seq // chunk, upcast, 0)

    for c, (_, dil) in enumerate(A_CONFIGS):
        so, sm, sl = stats[3 * c:3 * c + 3]
        n_blocks = seq // dil // BAND
        per_step = A_TILES_PER_STEP[dil]

        def rows(start, n, dil=dil):
            if dil == 1:
                return _rows(start, n, BAND)
            return pl.ds(start, n, stride=dil)

        def tiles(q_starts, back, n_keys, bias_cols, c=c, rows=rows, so=so, sm=sm, sl=sl):
            both = range(2)
            q_rows = [rows(qs, BAND) for qs in q_starts]
            k_rows = [rows(qs - back, n_keys) for qs in q_starts]
            ks = [kf[kr, :].astype(BF16) for kr in k_rows]
            vs = [vf[kr, :].astype(BF16) for kr in k_rows]
            qms = [[jnp.where(_head_mask((BAND, PAIR), hh), qf[qr, :], 0.0).astype(BF16) for hh in both]
                   for qr in q_rows]
            ss = [[_dot_nt(qms[u][hh], ks[u]) for hh in both] for u in range(len(q_starts))]
            ms, ls, ps = [], [], []
            for u in range(len(q_starts)):
                for hh in both:
                    s = ss[u][hh] + bias_ref[c, pair * 2 + hh, :, bias_cols]
                    m = jnp.max(s, axis=-1, keepdims=True)
                    p = jnp.exp(s - m)
                    ms.append(m)
                    ls.append(jnp.sum(p, axis=-1, keepdims=True))
                    ps.append(p.astype(BF16))
            os = [_dot(ps[2 * u + hh], vs[u]) for u in range(len(q_starts)) for hh in both]
            for u, qr in enumerate(q_rows):
                sm[qr, :] = jnp.where(hm0, ms[2 * u], ms[2 * u + 1])
                sl[qr, :] = jnp.where(hm0, ls[2 * u], ls[2 * u + 1])
                so[qr, :] = jnp.where(hm0, os[2 * u], os[2 * u + 1])

        first_cols = slice(BAND, 2 * BAND)
        all_cols = slice(0, 2 * BAND)
        span = dil * BAND

        def sweep(total, per_step, q_start_of, *args, tiles=tiles):
            if total <= per_step:
                tiles([q_start_of(u) for u in range(total)], *args)
                return

            def step(g, _):
                tiles([q_start_of(g * per_step + u) for u in range(per_step)], *args)
                return 0

            lax.fori_loop(0, total // per_step, step, 0)

        def later_start(idx, dil=dil, span=span):
            r = idx & (dil - 1)
            n = 1 + (idx >> (dil.bit_length() - 1))
            return r + n * span

        sweep(dil, per_step, lambda idx: idx, 0, BAND, first_cols)
        if n_blocks > 1:
            sweep(dil * (n_blocks - 1), per_step, later_start, span, 2 * BAND, all_cols)

    def merge(i, _):
        rows = _rows(i * BAND, BAND, BAND)
        ms = [stats[3 * c + 1][rows, :] for c in range(3)]
        mx = jnp.maximum(jnp.maximum(ms[0], ms[1]), ms[2])
        ws = [jnp.exp(m - mx) for m in ms]
        num = sum(w * stats[3 * c][rows, :] for c, w in enumerate(ws))
        den = sum(w * stats[3 * c + 2][rows, :] for c, w in enumerate(ws))
        o_ref[0, rows, :] = (num / den).astype(o_ref.dtype)
        return 0

    lax.fori_loop(0, seq // BAND, merge, 0)


def _mixer_a(proj, bias_a):
    b, seq, _ = proj.shape
    blk = COL_A // PAIR

    def col(j):
        return pl.BlockSpec((1, seq, PAIR), lambda i, p: (i, 0, blk + 2 * j + p))

    return pl.pallas_call(
        _mixer_a_kernel,
        grid=(b, 2),
        in_specs=[col(0), col(1), col(2),
                  pl.BlockSpec(bias_a.shape, lambda i, p: (0, 0, 0, 0))],
        out_specs=pl.BlockSpec((1, seq, PAIR), lambda i, p: (i, 0, p)),
        out_shape=jax.ShapeDtypeStruct((b, seq, GROUP_W), BF16),
        scratch_shapes=[pltpu.VMEM((seq, PAIR), F32)] * (3 + 3 * len(A_CONFIGS)),
        compiler_params=_cparams("parallel", "parallel"),
        name="mixer_a_dilated",
    )(proj, proj, proj, bias_a)


LOG2E = math.log2(math.e)


def _softplus(z):
    return jnp.log(1.0 + jnp.exp2(jnp.abs(z) * (-LOG2E))) + jnp.maximum(z, 0.0)


def _rev_cumsums(lms, tri):
    his = [lm.astype(BF16) for lm in lms]
    los = [(lm - hi.astype(F32)).astype(BF16) for lm, hi in zip(lms, his)]
    return [_dot(jnp.concatenate([hi, lo], axis=1), tri) for hi, lo in zip(his, los)]


def _mixer_b_kernel(q_ref, k_ref, v_ref, tri_ref, o_ref):
    seq = q_ref.shape[1]
    t = SB_BLOCK
    n_heads = 4
    past = (lax.broadcasted_iota(jnp.int32, (t, t), 1) < lax.broadcasted_iota(jnp.int32, (t, t), 0))
    masks = [_head_mask((t, PAIR), hh) for hh in range(2)]

    def lanes(h):
        return slice((h // 2) * PAIR, (h // 2 + 1) * PAIR)

    sel_row = lax.broadcasted_iota(jnp.int32, (8, PAIR), 0)
    sel_lane = lax.broadcasted_iota(jnp.int32, (8, PAIR), 1)
    head_sel = jnp.where(sel_row == sel_lane // HEAD_DIM, 1.0, 0.0).astype(BF16)

    def norm_step(i, best):
        out = []
        for p in range(2):
            kk = k_ref[0, _rows(i * t, t, t), p * PAIR:(p + 1) * PAIR].astype(F32)
            out.append(jnp.maximum(best[p], _dot_nt(head_sel, (kk * kk).astype(BF16))))
        return tuple(out)

    sq_max = lax.fori_loop(0, seq // t, norm_step, (jnp.zeros((8, t), F32),) * 2)
    k_max = [jnp.sqrt(jnp.max(sq_max[h // 2][h % 2:h % 2 + 1, :], axis=1, keepdims=True) * (1.0 + 2.0 ** -7))
             for h in range(n_heads)]

    def q_block(i, _):
        q_rows = _rows(i * t, t, t)
        heads = range(n_heads)
        qms = [jnp.where(masks[h % 2], q_ref[0, q_rows, lanes(h)] * SCALE, 0.0).astype(BF16) for h in heads]
        bounds = []
        for h in heads:
            qf = qms[h].astype(F32)
            bounds.append(jnp.sqrt(jnp.sum(qf * qf, axis=-1, keepdims=True)) * k_max[h])
        p_rows = _rows(jnp.maximum(i - 1, 0) * t, t, t)
        no_prev = jnp.where(i == 0, -NEG, 0.0)
        zs = [_dot_nt(qms[h], k_ref[0, q_rows, lanes(h)]) for h in heads]
        zp = [_dot_nt(qms[h], k_ref[0, p_rows, lanes(h)]) for h in heads]
        css = _rev_cumsums([jnp.where(past, _softplus(z), 0.0) for z in zs] + [_softplus(z) for z in zp],
                           tri_ref[...])
        css, csp = css[:n_heads], css[n_heads:]
        carries = [cs[:, 0:1] for cs in css]
        aa = [jnp.where(past, jnp.exp(zs[h] - css[h]), 0.0).astype(BF16) for h in heads]
        ap = [jnp.exp(zp[h] - csp[h] - (carries[h] + no_prev)).astype(BF16) for h in heads]
        accs = [_dot(aa[h], v_ref[0, q_rows, lanes(h)]) for h in heads]
        accp = [_dot(ap[h], v_ref[0, p_rows, lanes(h)]) for h in heads]
        accs = [accs[h] + accp[h] for h in heads]
        carries = [carries[h] + csp[h][:, 0:1] for h in heads]

        def walk(step0, rows, watch, accs, carries):
            qs = [qm[rows] for qm in qms]
            bs = [b[rows] for b in bounds]

            def live(carries):
                top = bs[0][watch] - carries[0][watch]
                for c, b in zip(carries[1:], bs[1:]):
                    top = jnp.maximum(top, b[watch] - c[watch])
                return jnp.max(top)

            def cond(state):
                step, top = state[0], state[1]
                return (step < i) & (top > SB_SKIP)

            def k_block(state):
                step = state[0]
                accs, carries = list(state[2:2 + n_heads]), list(state[2 + n_heads:])
                k_rows = _rows((i - 1 - step) * t, t, t)
                zs = [_dot_nt(qs[h], k_ref[0, k_rows, lanes(h)]) for h in heads]
                css = _rev_cumsums([_softplus(z) for z in zs], tri_ref[...])
                aa = [jnp.exp(zs[h] - css[h] - carries[h]).astype(BF16) for h in heads]
                pvs = [_dot(aa[h], v_ref[0, k_rows, lanes(h)]) for h in heads]
                accs = [accs[h] + pvs[h] for h in heads]
                carries = [carries[h] + css[h][:, 0:1] for h in heads]
                return (step + 1, live(carries), *accs, *carries)

            state = lax.while_loop(cond, k_block, (step0, live(carries), *accs, *carries))
            return state[0], list(state[2:2 + n_heads]), list(state[2 + n_heads:])

        upper, lower = slice(0, SB_UPPER), slice(SB_UPPER, t)
        step, accs, carries = walk(jnp.int32(1), slice(0, t), lower, accs, carries)
        _, accs_up, _ = walk(step, upper, upper, [a[upper] for a in accs], [c[upper] for c in carries])
        accs = [jnp.concatenate([accs_up[h], accs[h][lower]], axis=0) for h in heads]
        for p in range(2):
            o_ref[0, q_rows, p * PAIR:(p + 1) * PAIR] = jnp.where(
                masks[0], accs[2 * p], accs[2 * p + 1]).astype(o_ref.dtype)
        return 0

    lax.fori_loop(0, seq // t, q_block, 0)


def _mixer_b(proj, tri):
    b, seq, _ = proj.shape
    blk = COL_B // GROUP_W

    def col(j):
        return pl.BlockSpec((1, seq, GROUP_W), lambda i: (i, 0, blk + j))

    return pl.pallas_call(
        _mixer_b_kernel,
        grid=(b,),
        in_specs=[col(0), col(1), col(2), pl.BlockSpec(tri.shape, lambda i: (0, 0))],
        out_specs=pl.BlockSpec((1, seq, GROUP_W), lambda i: (i, 0, 0)),
        out_shape=jax.ShapeDtypeStruct((b, seq, GROUP_W), BF16),
        compiler_params=_cparams("parallel"),
        name="mixer_b_stickbreak",
    )(proj, proj, proj, tri)


C_TILES_PER_STEP = 5


def _mixer_c_kernel(sink_ref, q_ref, k_ref, v_ref, bias_ref, o_ref):
    seq = q_ref.shape[1]
    n_blocks = seq // BAND
    hm0 = _head_mask((BAND, PAIR), 0)

    for pb in range(2):
        lanes = slice(pb * PAIR, (pb + 1) * PAIR)

        def tiles(q_starts, back, n_keys, bias_cols, pb=pb, lanes=lanes):
            both = range(2)
            q_rows = [_rows(qs, BAND, BAND) for qs in q_starts]
            k_rows = [_rows(qs - back, n_keys, BAND) for qs in q_starts]
            qms = [[jnp.where(_head_mask((BAND, PAIR), hh), q_ref[0, qr, lanes] * SCALE, 0.0).astype(BF16)
                    for hh in both] for qr in q_rows]
            ss = [[_dot_nt(qms[u][hh], k_ref[0, k_rows[u], :]) for hh in both] for u in range(len(q_starts))]
            dens, ps = [], []
            for u in range(len(q_starts)):
                for hh in both:
                    head = hh * 2 + pb
                    sink = sink_ref[head]
                    s = ss[u][hh] + bias_ref[0, head, :, bias_cols]
                    m = jnp.maximum(jnp.max(s, axis=-1, keepdims=True), sink)
                    p = jnp.exp(s - m)
                    dens.append(jnp.sum(p, axis=-1, keepdims=True) + jnp.exp(sink - m))
                    ps.append(p.astype(BF16))
            os = [_dot(ps[2 * u + hh], v_ref[0, k_rows[u], :]) for u in range(len(q_starts)) for hh in both]
            for u, qr in enumerate(q_rows):
                o_ref[0, qr, lanes] = jnp.where(hm0, os[2 * u] / dens[2 * u],
                                                os[2 * u + 1] / dens[2 * u + 1]).astype(o_ref.dtype)

        tiles([0], 0, BAND, slice(BAND, 2 * BAND))

        def later_step(g, _, tiles=tiles):
            tiles([(1 + g * C_TILES_PER_STEP + u) * BAND for u in range(C_TILES_PER_STEP)],
                  BAND, 2 * BAND, slice(0, 2 * BAND))
            return 0

        lax.fori_loop(0, (n_blocks - 1) // C_TILES_PER_STEP, later_step, 0)


def _mixer_c(proj, bias_c, sinks):
    b, seq, _ = proj.shape
    return pl.pallas_call(
        _mixer_c_kernel,
        grid=(b,),
        in_specs=[pl.BlockSpec(memory_space=pltpu.SMEM),
                  pl.BlockSpec((1, seq, GROUP_W), lambda i: (i, 0, COL_CQ // GROUP_W)),
                  pl.BlockSpec((1, seq, PAIR), lambda i: (i, 0, COL_CK // PAIR)),
                  pl.BlockSpec((1, seq, PAIR), lambda i: (i, 0, COL_CV // PAIR)),
                  pl.BlockSpec(bias_c.shape, lambda i: (0, 0, 0, 0))],
        out_specs=pl.BlockSpec((1, seq, GROUP_W), lambda i: (i, 0, 0)),
        out_shape=jax.ShapeDtypeStruct((b, seq, GROUP_W), BF16),
        compiler_params=_cparams("parallel"),
        name="mixer_c_swa_sink",
    )(sinks, proj, proj, proj, bias_c)


def _mixer_d_kernel(q_ref, k_ref, v_ref, bias_ref, o_ref, vt_scr, sel_scr):
    seq = q_ref.shape[1]
    t = MOBA_BLOCK
    n_blk = seq // t
    n_heads = 4
    feat = lax.broadcasted_iota(jnp.int32, (PAIR, t), 0)
    fmasks = [feat < HEAD_DIM, feat >= HEAD_DIM]
    blk_id = lax.broadcasted_iota(jnp.int32, (n_blk, t), 0)

    def lanes(h):
        return slice((h // 2) * PAIR, (h // 2 + 1) * PAIR)

    km_hi, km_lo = [], []
    for p in range(2):
        kmean = jnp.concatenate(
            [jnp.sum(k_ref[0, j * t:(j + 1) * t, p * PAIR:(p + 1) * PAIR].astype(F32), axis=0, keepdims=True)
             for j in range(n_blk)], axis=0) * (1.0 / t)
        hi = kmean.astype(BF16)
        km_hi.append(hi)
        km_lo.append((kmean - hi.astype(F32)).astype(BF16))
        for j in range(n_blk):
            vt_scr[p, j, 0:PAIR, :] = v_ref[0, j * t:(j + 1) * t, p * PAIR:(p + 1) * PAIR].astype(F32).T.astype(BF16)
            vt_scr[p, j, PAIR:, :] = jnp.ones((MOBA_SUM_ROWS, t), BF16)

    def q_block(i, _):
        q_rows = _rows(i * t, t, t)
        heads = range(n_heads)
        qms = [jnp.where(fmasks[h % 2], q_ref[0, q_rows, lanes(h)].astype(F32).T, 0.0).astype(BF16)
               for h in heads]
        qts = [qm * SCALE for qm in qms]
        gates = [_dot(km_hi[h // 2], qms[h]) + _dot(km_lo[h // 2], qms[h]) for h in heads]
        for h in heads:
            gate = jnp.where(blk_id < i, gates[h], NEG)
            rank = jnp.zeros((n_blk, t), jnp.int32)
            for mth in range(n_blk):
                gm = gate[mth:mth + 1, :]
                ahead = (gm > gate) | ((gm == gate) & (mth < blk_id))
                rank = rank + ahead.astype(jnp.int32)
            chosen = ((rank < MOBA_TOPK) & (blk_id < i)) | (blk_id == i)
            sel_scr[h] = jnp.where(chosen, 0.0, NEG)
        state = []
        for h in heads:
            state += [jnp.full((1, t), NEG, F32), jnp.zeros((PAIR + MOBA_SUM_ROWS, t), F32)]

        def k_blocks(js, state):
            k_rows = [_rows(j * t, t, t) for j in js]
            ss = [[_dot(k_ref[0, kr, lanes(h)], qts[h]) for kr in k_rows] for h in heads]
            ms, alphas, ps = [], [], []
            for h in heads:
                m = state[2 * h]
                s = [ss[h][b] + bias_ref[i - j, h] + sel_scr[h, pl.ds(j, 1), :] for b, j in enumerate(js)]
                m_new = m
                for sb in s:
                    m_new = jnp.maximum(m_new, jnp.max(sb, axis=0, keepdims=True))
                ms.append(m_new)
                alphas.append(jnp.exp(m - m_new))
                ps.append(jnp.concatenate([jnp.exp((sb - m_new).astype(BF16)) for sb in s], axis=0))
            pvs = [_dot(jnp.concatenate([vt_scr[h // 2, j] for j in js], axis=1), ps[h]) for h in heads]
            new_state = []
            for h in heads:
                new_state += [ms[h], state[2 * h + 1] * alphas[h] + pvs[h]]
            return tuple(new_state)

        def block_at(k):
            return jnp.where(k == 0, i, k - 1)

        odd = (i + 1) & 1
        state = lax.fori_loop(0, odd, lambda _, st: k_blocks([i], st), tuple(state))
        state = lax.fori_loop(0, (i + 1) >> 1,
                              lambda g, st: k_blocks([block_at(odd + 2 * g), block_at(odd + 2 * g + 1)], st), state)
        outs = [state[2 * h + 1][:PAIR] / state[2 * h + 1][PAIR:PAIR + 1] for h in heads]
        for p in range(2):
            o_ref[0, q_rows, p * PAIR:(p + 1) * PAIR] = jnp.where(
                fmasks[0], outs[2 * p], outs[2 * p + 1]).T.astype(o_ref.dtype)
        return 0

    lax.fori_loop(0, n_blk, q_block, 0)


def _mixer_d(proj, bias_d):
    b, seq, _ = proj.shape
    blk = COL_D // GROUP_W
    n_blk = seq // MOBA_BLOCK

    def col(j):
        return pl.BlockSpec((1, seq, GROUP_W), lambda i: (i, 0, blk + j))

    return pl.pallas_call(
        _mixer_d_kernel,
        grid=(b,),
        in_specs=[col(0), col(1), col(2),
                  pl.BlockSpec(bias_d.shape, lambda i: (0, 0, 0, 0))],
        out_specs=pl.BlockSpec((1, seq, GROUP_W), lambda i: (i, 0, 0)),
        out_shape=jax.ShapeDtypeStruct((b, seq, GROUP_W), BF16),
        scratch_shapes=[pltpu.VMEM((2, n_blk, PAIR + MOBA_SUM_ROWS, MOBA_BLOCK), BF16),
                        pltpu.VMEM((4, n_blk, MOBA_BLOCK), F32)],
        compiler_params=_cparams("parallel"),
        name="mixer_d_moba",
    )(proj, proj, proj, bias_d)


def _mix_out_rows(x, ys, g_ref, w_ref):
    acc = x
    for gi, y in enumerate(ys):
        cols = slice(gi * GROUP_W, (gi + 1) * GROUP_W)
        yn = _rms(y.astype(F32), g_ref[:, cols]).astype(BF16)
        acc = acc + _dot(yn, w_ref[cols, :])
    return acc


def _cross_rows(xs, g_ref, wq_ref, kv_ref, wo_ref):
    tm = xs[0].shape[0]
    heads = range(4)
    work = [(c, h) for c in range(len(xs)) for h in heads]

    def lanes(h, lo=0):
        return slice(lo + (h // 2) * PAIR, lo + (h // 2 + 1) * PAIR)

    hs = [_rms(x, g_ref[...]).astype(BF16) for x in xs]
    qs = [(_dot(h, wq_ref[...]) * SCALE).astype(BF16) for h in hs]
    qms = [jnp.where(_head_mask((tm, PAIR), h % 2), qs[c][:, lanes(h)], 0.0).astype(BF16) for c, h in work]
    ss = [_dot_nt(qms[w], kv_ref[:, lanes(h)]) for w, (c, h) in enumerate(work)]
    ls, ps = [], []
    for s in ss:
        m = jnp.max(s, axis=-1, keepdims=True)
        p = jnp.exp(s - m)
        ls.append(jnp.sum(p, axis=-1, keepdims=True))
        ps.append(p.astype(BF16))
    pvs = [_dot(ps[w], kv_ref[:, lanes(h, X_W)]) for w, (c, h) in enumerate(work)]
    outs = []
    for c, x in enumerate(xs):
        os = [jnp.where(_head_mask((tm, PAIR), 0), pvs[4 * c + 2 * p] / ls[4 * c + 2 * p],
                        pvs[4 * c + 2 * p + 1] / ls[4 * c + 2 * p + 1]).astype(BF16) for p in range(2)]
        outs.append(x + _dot(os[0], wo_ref[0:PAIR, :]) + _dot(os[1], wo_ref[PAIR:2 * PAIR, :]))
    return outs


POST_CHUNK = 512


def _post_kernel(x_ref, ya_ref, yb_ref, yc_ref, yd_ref, gg_ref, wout_ref, gc_ref, wq_ref, kv_ref, wo_ref,
                 gm_ref, wu_ref, wd_ref, gf_ref, o_ref, xn_scr, acc_scr, *, final_norm):
    j = pl.program_id(1)

    @pl.when(j == 0)
    def _():
        chunks = [pl.ds(c * POST_CHUNK, POST_CHUNK) for c in range(x_ref.shape[0] // POST_CHUNK)]
        for rows in chunks:
            ys = [y_ref[rows, :] for y_ref in (ya_ref, yb_ref, yc_ref, yd_ref)]
            acc_scr[rows, :] = _mix_out_rows(x_ref[rows, :], ys, gg_ref, wout_ref)
        xs = _cross_rows([acc_scr[rows, :] for rows in chunks], gc_ref, wq_ref, kv_ref, wo_ref)
        for rows, x in zip(chunks, xs):
            xn_scr[rows, :] = _rms(x, gm_ref[...]).astype(BF16)
            acc_scr[rows, :] = x

    h = _dot(xn_scr[...], wu_ref[...])
    h = jnp.square(jnp.maximum(h, 0.0)).astype(BF16)
    acc_scr[...] += _dot(h, wd_ref[...])

    @pl.when(j == pl.num_programs(1) - 1)
    def _():
        y = acc_scr[...]
        if final_norm:
            y = _rms(y, gf_ref[...])
        o_ref[...] = y


def _post(x, ys, g_group, w_out, g_cross, wq, kv, wo, g_mlp, wu, wd, g_final, final_norm, seq, tm, tf):
    t, d = x.shape
    ff = wu.shape[1]
    mem_len = kv.shape[1]
    tiles_per_seq = seq // tm

    def const(shape):
        return pl.BlockSpec(shape, lambda i, j: (0,) * len(shape))

    y_spec = pl.BlockSpec((tm, GROUP_W), lambda i, j: (i, 0))
    return pl.pallas_call(
        functools.partial(_post_kernel, final_norm=final_norm),
        grid=(t // tm, ff // tf),
        in_specs=[pl.BlockSpec((tm, d), lambda i, j: (i, 0)), y_spec, y_spec, y_spec, y_spec,
                  const((1, 4 * GROUP_W)), const((4 * GROUP_W, d)),
                  const((1, d)), const((d, X_W)),
                  pl.BlockSpec((None, mem_len, 2 * X_W), lambda i, j: (i // tiles_per_seq, 0, 0)),
                  const((X_W, d)),
                  const((1, d)),
                  pl.BlockSpec((d, tf), lambda i, j: (0, j)),
                  pl.BlockSpec((tf, d), lambda i, j: (j, 0)),
                  const((1, d))],
        out_specs=pl.BlockSpec((tm, d), lambda i, j: (i, 0)),
        out_shape=jax.ShapeDtypeStruct((t, d), F32),
        scratch_shapes=[pltpu.VMEM((tm, d), BF16), pltpu.VMEM((tm, d), F32)],
        compiler_params=pltpu.CompilerParams(dimension_semantics=("parallel", "arbitrary"),
                                             vmem_limit_bytes=POST_VMEM_LIMIT),
        name="post_mixer_block",
    )(x, *ys, g_group.reshape(1, -1), w_out, g_cross.reshape(1, d), wq, kv, wo,
      g_mlp.reshape(1, d), wu, wd, g_final.reshape(1, d))


def _permute_c_heads(a, axis, lo):
    parts = [lax.slice_in_dim(a, 0, lo, axis=axis)]
    parts += [lax.slice_in_dim(a, lo + h * HEAD_DIM, lo + (h + 1) * HEAD_DIM, axis=axis) for h in C_HEAD_ORDER]
    parts.append(lax.slice_in_dim(a, lo + GROUP_W, a.shape[axis], axis=axis))
    return jnp.concatenate(parts, axis=axis)


def kernel(x, mem, rel_table, g_mix, w_in, g_group, sinks, w_out, g_cross, g_mem, w_xq, w_xkv, w_xo,
           g_mlp, w_up, w_down, g_final):
    b, seq, d = x.shape
    mem_len = mem.shape[1]
    depth = w_in.shape[0]
    tokens = b * seq

    bkt_a, bkt_c, bkt_d = _static_buckets()
    tab_t = rel_table.astype(F32).T
    bias_a = _rel_bias(tab_t, jnp.asarray(bkt_a), A_BIAS_LO)
    bias_c = _rel_bias(tab_t, jnp.asarray(bkt_c), C_BIAS_LO)
    bias_d = _rel_bias(tab_t, jnp.asarray(bkt_d), D_BIAS_LO)
    tri = np.tril(np.ones((SB_BLOCK, SB_BLOCK), np.float32))
    tri = jnp.asarray(np.concatenate([tri, tri], axis=0), dtype=BF16)

    c_out_lo = 2 * GROUP_W
    xf = x.reshape(tokens, d)
    mem_f = mem.reshape(b * mem_len, d)
    for l in range(depth):
        w_in_l = _permute_c_heads(w_in[l], 1, COL_CQ).astype(BF16)
        g_group_l = _permute_c_heads(g_group[l], 0, c_out_lo)
        w_out_l = _permute_c_heads(w_out[l], 0, c_out_lo).astype(BF16)

        proj = _norm_matmul(xf, g_mix[l], w_in_l, tm=512).reshape(b, seq, IN_WIDTH)
        ya = _mixer_a(proj, bias_a)
        yb = _mixer_b(proj, tri)
        yc = _mixer_c(proj, bias_c, sinks[l].astype(F32))
        yd = _mixer_d(proj, bias_d)
        ys = [y.reshape(tokens, GROUP_W) for y in (ya, yb, yc, yd)]
        kv = _norm_matmul(mem_f, g_mem[l], w_xkv[l].astype(BF16), tm=512).reshape(b, mem_len, 2 * X_W)
        xf = _post(xf, ys, g_group_l, w_out_l, g_cross[l], w_xq[l].astype(BF16), kv, w_xo[l].astype(BF16),
                   g_mlp[l], w_up[l].astype(BF16), w_down[l].astype(BF16), g_final,
                   final_norm=(l == depth - 1), seq=seq, tm=1024, tf=1024)
    return xf.reshape(b, seq, d)
```

```python
import functools
import math

import numpy as np
import jax
import jax.numpy as jnp
from jax import lax
from jax.experimental import pallas as pl
from jax.experimental.pallas import tpu as pltpu

F32 = jnp.float32
BF16 = jnp.bfloat16

HEAD_DIM = 64
PAIR = 2 * HEAD_DIM
NORM_EPS = 1e-6
NEG = -1e30
SCALE = HEAD_DIM ** -0.5
A_CONFIGS = ((128, 1), (512, 4), (2048, 16))
BAND = 128
C_WINDOW = 128
SB_BLOCK = 256
SB_SKIP = -100.0
SB_UPPER = 176
MOBA_BLOCK = 256
MOBA_TOPK = 3
MOBA_SUM_ROWS = 16
REL_BUCKETS = 32
REL_MAX_DIST = 2048
A_BIAS_LO, C_BIAS_LO, D_BIAS_LO = 0, 4, 8
GROUP_W = 256
X_W = 256
COL_A, COL_B, COL_CQ, COL_CK, COL_CV, COL_D = 0, 768, 1536, 1792, 1920, 2048
IN_WIDTH = 2816
C_HEAD_ORDER = (0, 2, 1, 3)

VMEM_LIMIT = 48 * 1024 * 1024
POST_VMEM_LIMIT = 56 * 1024 * 1024


def _cparams(*sem):
    return pltpu.CompilerParams(dimension_semantics=sem, vmem_limit_bytes=VMEM_LIMIT)


def _dot(a, b):
    return lax.dot_general(a, b, (((1,), (0,)), ((), ())), preferred_element_type=F32)


def _dot_nt(a, b):
    return lax.dot_general(a, b, (((1,), (1,)), ((), ())), preferred_element_type=F32)


def _head_mask(shape, hh):
    lane = lax.broadcasted_iota(jnp.int32, shape, len(shape) - 1)
    return (lane < HEAD_DIM) if hh == 0 else (lane >= HEAD_DIM)


def _rows(start, size, align):
    if isinstance(start, int):
        return pl.ds(start, size)
    return pl.ds(pl.multiple_of(start, align), size)


def _bucket_np(dist):
    max_exact = REL_BUCKETS // 2
    n = np.maximum(dist, 0)
    nf = np.maximum(n, 1).astype(np.float64)
    large = max_exact + (np.log(nf / max_exact) / math.log(REL_MAX_DIST / max_exact)
                         * (REL_BUCKETS - max_exact)).astype(np.int32)
    large = np.minimum(large, REL_BUCKETS - 1)
    return np.where(n < max_exact, n, large).astype(np.int32)


def _static_buckets():
    qi = np.arange(BAND)[:, None]
    ki = np.arange(2 * BAND)[None, :]
    dist = qi + BAND - ki
    bkt_a = []
    for window, dil in A_CONFIGS:
        valid = (dist >= 0) & (dist <= window // dil)
        bkt_a.append(np.where(valid, _bucket_np(dist * dil), -1))
    valid_c = (dist >= 0) & (dist < C_WINDOW)
    bkt_c = np.where(valid_c, _bucket_np(dist), -1)[None]
    qd = np.arange(MOBA_BLOCK)[None, :]
    kd = np.arange(MOBA_BLOCK)[:, None]
    bkt_d = []
    for delta in range(8):
        dd = delta * MOBA_BLOCK + qd - kd
        bkt_d.append(np.where(dd >= 0, _bucket_np(dd), -1))
    return (np.stack(bkt_a).astype(np.int32), bkt_c.astype(np.int32),
            np.stack(bkt_d).astype(np.int32))


def _rel_bias_kernel(tab_ref, bkt_ref, o_ref, *, head_lo):
    h = head_lo + pl.program_id(1)
    b = bkt_ref[0]
    acc = jnp.full(b.shape, NEG, F32)
    for k in range(REL_BUCKETS):
        acc = jnp.where(b == k, tab_ref[h, k], acc)
    o_ref[0, 0] = acc


def _rel_bias(tab_t, buckets, head_lo):
    n, r, c = buckets.shape
    return pl.pallas_call(
        functools.partial(_rel_bias_kernel, head_lo=head_lo),
        grid=(n, 4),
        in_specs=[pl.BlockSpec(memory_space=pltpu.SMEM),
                  pl.BlockSpec((1, r, c), lambda i, h: (i, 0, 0))],
        out_specs=pl.BlockSpec((1, 1, r, c), lambda i, h: (i, h, 0, 0)),
        out_shape=jax.ShapeDtypeStruct((n, 4, r, c), F32),
        compiler_params=_cparams("arbitrary", "arbitrary"),
        name="rel_bias",
    )(tab_t, buckets)


def _rms(x, g):
    ms = jnp.mean(x * x, axis=-1, keepdims=True)
    return x * lax.rsqrt(ms + NORM_EPS) * g


def _norm_matmul_kernel(x_ref, g_ref, w_ref, o_ref):
    xn = _rms(x_ref[...], g_ref[...]).astype(BF16)
    o_ref[...] = _dot(xn, w_ref[...]).astype(o_ref.dtype)


def _norm_matmul(x, g, w, tm):
    t, d = x.shape
    n = w.shape[1]
    return pl.pallas_call(
        _norm_matmul_kernel,
        grid=(t // tm,),
        in_specs=[pl.BlockSpec((tm, d), lambda i: (i, 0)),
                  pl.BlockSpec((1, d), lambda i: (0, 0)),
                  pl.BlockSpec((d, n), lambda i: (0, 0))],
        out_specs=pl.BlockSpec((tm, n), lambda i: (i, 0)),
        out_shape=jax.ShapeDtypeStruct((t, n), BF16),
        compiler_params=_cparams("parallel"),
        name="norm_matmul",
    )(x, g.reshape(1, d), w)


A_TILES_PER_STEP = {1: 15, 4: 12, 16: 8}


def _mixer_a_kernel(q_ref, k_ref, v_ref, bias_ref, o_ref, qf, kf, vf, *stats):
    seq = q_ref.shape[1]
    pair = pl.program_id(1)
    chunk = 256
    hm0 = _head_mask((BAND, PAIR), 0)

    def upcast(i, _):
        rows = _rows(i * chunk, chunk, chunk)
        qf[rows, :] = q_ref[0, rows, :].astype(F32) * SCALE
        kf[rows, :] = k_ref[0, rows, :].astype(F32)
        vf[rows, :] = v_ref[0, rows, :].astype(F32)
        return 0

    lax.fori_loop(0, seq // chunk, upcast, 0)

    for c, (_, dil) in enumerate(A_CONFIGS):
        so, sm, sl = stats[3 * c:3 * c + 3]
        n_blocks = seq // dil // BAND
        per_step = A_TILES_PER_STEP[dil]

        def rows(start, n, dil=dil):
            if dil == 1:
                return _rows(start, n, BAND)
            return pl.ds(start, n, stride=dil)

        def tiles(q_starts, back, n_keys, bias_cols, c=c, rows=rows, so=so, sm=sm, sl=sl):
            both = range(2)
            q_rows = [rows(qs, BAND) for qs in q_starts]
            k_rows = [rows(qs - back, n_keys) for qs in q_starts]
            ks = [kf[kr, :].astype(BF16) for kr in k_rows]
            vs = [vf[kr, :].astype(BF16) for kr in k_rows]
            qms = [[jnp.where(_head_mask((BAND, PAIR), hh), qf[qr, :], 0.0).astype(BF16) for hh in both]
                   for qr in q_rows]
            ss = [[_dot_nt(qms[u][hh], ks[u]) for hh in both] for u in range(len(q_starts))]
            ms, ls, ps = [], [], []
            for u in range(len(q_starts)):
                for hh in both:
                    s = ss[u][hh] + bias_ref[c, pair * 2 + hh, :, bias_cols]
                    m = jnp.max(s, axis=-1, keepdims=True)
                    p = jnp.exp(s - m)
                    ms.append(m)
                    ls.append(jnp.sum(p, axis=-1, keepdims=True))
                    ps.append(p.astype(BF16))
            os = [_dot(ps[2 * u + hh], vs[u]) for u in range(len(q_starts)) for hh in both]
            for u, qr in enumerate(q_rows):
                sm[qr, :] = jnp.where(hm0, ms[2 * u], ms[2 * u + 1])
                sl[qr, :] = jnp.where(hm0, ls[2 * u], ls[2 * u + 1])
                so[qr, :] = jnp.where(hm0, os[2 * u], os[2 * u + 1])

        first_cols = slice(BAND, 2 * BAND)
        all_cols = slice(0, 2 * BAND)
        span = dil * BAND

        def sweep(total, per_step, q_start_of, *args, tiles=tiles):
            if total <= per_step:
                tiles([q_start_of(u) for u in range(total)], *args)
                return

            def step(g, _):
                tiles([q_start_of(g * per_step + u) for u in range(per_step)], *args)
                return 0

            lax.fori_loop(0, total // per_step, step, 0)

        def later_start(idx, dil=dil, span=span):
            r = idx & (dil - 1)
            n = 1 + (idx >> (dil.bit_length() - 1))
            return r + n * span

        sweep(dil, per_step, lambda idx: idx, 0, BAND, first_cols)
        if n_blocks > 1:
            sweep(dil * (n_blocks - 1), per_step, later_start, span, 2 * BAND, all_cols)

    def merge(i, _):
        rows = _rows(i * BAND, BAND, BAND)
        ms = [stats[3 * c + 1][rows, :] for c in range(3)]
        mx = jnp.maximum(jnp.maximum(ms[0], ms[1]), ms[2])
        ws = [jnp.exp(m - mx) for m in ms]
        num = sum(w * stats[3 * c][rows, :] for c, w in enumerate(ws))
        den = sum(w * stats[3 * c + 2][rows, :] for c, w in enumerate(ws))
        o_ref[0, rows, :] = (num / den).astype(o_ref.dtype)
        return 0

    lax.fori_loop(0, seq // BAND, merge, 0)


def _mixer_a(proj, bias_a):
    b, seq, _ = proj.shape
    blk = COL_A // PAIR

    def col(j):
        return pl.BlockSpec((1, seq, PAIR), lambda i, p: (i, 0, blk + 2 * j + p))

    return pl.pallas_call(
        _mixer_a_kernel,
        grid=(b, 2),
        in_specs=[col(0), col(1), col(2),
                  pl.BlockSpec(bias_a.shape, lambda i, p: (0, 0, 0, 0))],
        out_specs=pl.BlockSpec((1, seq, PAIR), lambda i, p: (i, 0, p)),
        out_shape=jax.ShapeDtypeStruct((b, seq, GROUP_W), BF16),
        scratch_shapes=[pltpu.VMEM((seq, PAIR), F32)] * (3 + 3 * len(A_CONFIGS)),
        compiler_params=_cparams("parallel", "parallel"),
        name="mixer_a_dilated",
    )(proj, proj, proj, bias_a)


LOG2E = math.log2(math.e)


def _softplus(z):
    return jnp.log(1.0 + jnp.exp2(jnp.abs(z) * (-LOG2E))) + jnp.maximum(z, 0.0)


def _rev_cumsums(lms, tri):
    his = [lm.astype(BF16) for lm in lms]
    los = [(lm - hi.astype(F32)).astype(BF16) for lm, hi in zip(lms, his)]
    return [_dot(jnp.concatenate([hi, lo], axis=1), tri) for hi, lo in zip(his, los)]


def _mixer_b_kernel(q_ref, k_ref, v_ref, tri_ref, o_ref):
    seq = q_ref.shape[1]
    t = SB_BLOCK
    n_heads = 4
    past = (lax.broadcasted_iota(jnp.int32, (t, t), 1) < lax.broadcasted_iota(jnp.int32, (t, t), 0))
    masks = [_head_mask((t, PAIR), hh) for hh in range(2)]

    def lanes(h):
        return slice((h // 2) * PAIR, (h // 2 + 1) * PAIR)

    sel_row = lax.broadcasted_iota(jnp.int32, (8, PAIR), 0)
    sel_lane = lax.broadcasted_iota(jnp.int32, (8, PAIR), 1)
    head_sel = jnp.where(sel_row == sel_lane // HEAD_DIM, 1.0, 0.0).astype(BF16)

    def norm_step(i, best):
        out = []
        for p in range(2):
            kk = k_ref[0, _rows(i * t, t, t), p * PAIR:(p + 1) * PAIR].astype(F32)
            out.append(jnp.maximum(best[p], _dot_nt(head_sel, (kk * kk).astype(BF16))))
        return tuple(out)

    sq_max = lax.fori_loop(0, seq // t, norm_step, (jnp.zeros((8, t), F32),) * 2)
    k_max = [jnp.sqrt(jnp.max(sq_max[h // 2][h % 2:h % 2 + 1, :], axis=1, keepdims=True) * (1.0 + 2.0 ** -7))
             for h in range(n_heads)]

    def q_block(i, _):
        q_rows = _rows(i * t, t, t)
        heads = range(n_heads)
        qms = [jnp.where(masks[h % 2], q_ref[0, q_rows, lanes(h)] * SCALE, 0.0).astype(BF16) for h in heads]
        bounds = []
        for h in heads:
            qf = qms[h].astype(F32)
            bounds.append(jnp.sqrt(jnp.sum(qf * qf, axis=-1, keepdims=True)) * k_max[h])
        p_rows = _rows(jnp.maximum(i - 1, 0) * t, t, t)
        no_prev = jnp.where(i == 0, -NEG, 0.0)
        zs = [_dot_nt(qms[h], k_ref[0, q_rows, lanes(h)]) for h in heads]
        zp = [_dot_nt(qms[h], k_ref[0, p_rows, lanes(h)]) for h in heads]
        css = _rev_cumsums([jnp.where(past, _softplus(z), 0.0) for z in zs] + [_softplus(z) for z in zp],
                           tri_ref[...])
        css, csp = css[:n_heads], css[n_heads:]
        carries = [cs[:, 0:1] for cs in css]
        aa = [jnp.where(past, jnp.exp(zs[h] - css[h]), 0.0).astype(BF16) for h in heads]
        ap = [jnp.exp(zp[h] - csp[h] - (carries[h] + no_prev)).astype(BF16) for h in heads]
        accs = [_dot(aa[h], v_ref[0, q_rows, lanes(h)]) for h in heads]
        accp = [_dot(ap[h], v_ref[0, p_rows, lanes(h)]) for h in heads]
        accs = [accs[h] + accp[h] for h in heads]
        carries = [carries[h] + csp[h][:, 0:1] for h in heads]

        def walk(step0, rows, watch, accs, carries):
            qs = [qm[rows] for qm in qms]
            bs = [b[rows] for b in bounds]

            def live(carries):
                top = bs[0][watch] - carries[0][watch]
                for c, b in zip(carries[1:], bs[1:]):
                    top = jnp.maximum(top, b[watch] - c[watch])
                return jnp.max(top)

            def cond(state):
                step, top = state[0], state[1]
                return (step < i) & (top > SB_SKIP)

            def k_block(state):
                step = state[0]
                accs, carries = list(state[2:2 + n_heads]), list(state[2 + n_heads:])
                k_rows = _rows((i - 1 - step) * t, t, t)
                zs = [_dot_nt(qs[h], k_ref[0, k_rows, lanes(h)]) for h in heads]
                css = _rev_cumsums([_softplus(z) for z in zs], tri_ref[...])
                aa = [jnp.exp(zs[h] - css[h] - carries[h]).astype(BF16) for h in heads]
                pvs = [_dot(aa[h], v_ref[0, k_rows, lanes(h)]) for h in heads]
                accs = [accs[h] + pvs[h] for h in heads]
                carries = [carries[h] + css[h][:, 0:1] for h in heads]
                return (step + 1, live(carries), *accs, *carries)

            state = lax.while_loop(cond, k_block, (step0, live(carries), *accs, *carries))
            return state[0], list(state[2:2 + n_heads]), list(state[2 + n_heads:])

        upper, lower = slice(0, SB_UPPER), slice(SB_UPPER, t)
        step, accs, carries = walk(jnp.int32(1), slice(0, t), lower, accs, carries)
        _, accs_up, _ = walk(step, upper, upper, [a[upper] for a in accs], [c[upper] for c in carries])
        accs = [jnp.concatenate([accs_up[h], accs[h][lower]], axis=0) for h in heads]
        for p in range(2):
            o_ref[0, q_rows, p * PAIR:(p + 1) * PAIR] = jnp.where(
                masks[0], accs[2 * p], accs[2 * p + 1]).astype(o_ref.dtype)
        return 0

    lax.fori_loop(0, seq // t, q_block, 0)


def _mixer_b(proj, tri):
    b, seq, _ = proj.shape
    blk = COL_B // GROUP_W

    def col(j):
        return pl.BlockSpec((1, seq, GROUP_W), lambda i: (i, 0, blk + j))

    return pl.pallas_call(
        _mixer_b_kernel,
        grid=(b,),
        in_specs=[col(0), col(1), col(2), pl.BlockSpec(tri.shape, lambda i: (0, 0))],
        out_specs=pl.BlockSpec((1, seq, GROUP_W), lambda i: (i, 0, 0)),
        out_shape=jax.ShapeDtypeStruct((b, seq, GROUP_W), BF16),
        compiler_params=_cparams("parallel"),
        name="mixer_b_stickbreak",
    )(proj, proj, proj, tri)


C_TILES_PER_STEP = 5


def _mixer_c_kernel(sink_ref, q_ref, k_ref, v_ref, bias_ref, o_ref):
    seq = q_ref.shape[1]
    n_blocks = seq // BAND
    hm0 = _head_mask((BAND, PAIR), 0)

    for pb in range(2):
        lanes = slice(pb * PAIR, (pb + 1) * PAIR)

        def tiles(q_starts, back, n_keys, bias_cols, pb=pb, lanes=lanes):
            both = range(2)
            q_rows = [_rows(qs, BAND, BAND) for qs in q_starts]
            k_rows = [_rows(qs - back, n_keys, BAND) for qs in q_starts]
            qms = [[jnp.where(_head_mask((BAND, PAIR), hh), q_ref[0, qr, lanes] * SCALE, 0.0).astype(BF16)
                    for hh in both] for qr in q_rows]
            ss = [[_dot_nt(qms[u][hh], k_ref[0, k_rows[u], :]) for hh in both] for u in range(len(q_starts))]
            dens, ps = [], []
            for u in range(len(q_starts)):
                for hh in both:
                    head = hh * 2 + pb
                    sink = sink_ref[head]
                    s = ss[u][hh] + bias_ref[0, head, :, bias_cols]
                    m = jnp.maximum(jnp.max(s, axis=-1, keepdims=True), sink)
                    p = jnp.exp(s - m)
                    dens.append(jnp.sum(p, axis=-1, keepdims=True) + jnp.exp(sink - m))
                    ps.append(p.astype(BF16))
            os = [_dot(ps[2 * u + hh], v_ref[0, k_rows[u], :]) for u in range(len(q_starts)) for hh in both]
            for u, qr in enumerate(q_rows):
                o_ref[0, qr, lanes] = jnp.where(hm0, os[2 * u] / dens[2 * u],
                                                os[2 * u + 1] / dens[2 * u + 1]).astype(o_ref.dtype)

        tiles([0], 0, BAND, slice(BAND, 2 * BAND))

        def later_step(g, _, tiles=tiles):
            tiles([(1 + g * C_TILES_PER_STEP + u) * BAND for u in range(C_TILES_PER_STEP)],
                  BAND, 2 * BAND, slice(0, 2 * BAND))
            return 0

        lax.fori_loop(0, (n_blocks - 1) // C_TILES_PER_STEP, later_step, 0)


def _mixer_c(proj, bias_c, sinks):
    b, seq, _ = proj.shape
    return pl.pallas_call(
        _mixer_c_kernel,
        grid=(b,),
        in_specs=[pl.BlockSpec(memory_space=pltpu.SMEM),
                  pl.BlockSpec((1, seq, GROUP_W), lambda i: (i, 0, COL_CQ // GROUP_W)),
                  pl.BlockSpec((1, seq, PAIR), lambda i: (i, 0, COL_CK // PAIR)),
                  pl.BlockSpec((1, seq, PAIR), lambda i: (i, 0, COL_CV // PAIR)),
                  pl.BlockSpec(bias_c.shape, lambda i: (0, 0, 0, 0))],
        out_specs=pl.BlockSpec((1, seq, GROUP_W), lambda i: (i, 0, 0)),
        out_shape=jax.ShapeDtypeStruct((b, seq, GROUP_W), BF16),
        compiler_params=_cparams("parallel"),
        name="mixer_c_swa_sink",
    )(sinks, proj, proj, proj, bias_c)


def _mixer_d_kernel(q_ref, k_ref, v_ref, bias_ref, o_ref, vt_scr, sel_scr):
    seq = q_ref.shape[1]
    t = MOBA_BLOCK
    n_blk = seq // t
    n_heads = 4
    feat = lax.broadcasted_iota(jnp.int32, (PAIR, t), 0)
    fmasks = [feat < HEAD_DIM, feat >= HEAD_DIM]
    blk_id = lax.broadcasted_iota(jnp.int32, (n_blk, t), 0)

    def lanes(h):
        return slice((h // 2) * PAIR, (h // 2 + 1) * PAIR)

    km_hi, km_lo = [], []
    for p in range(2):
        kmean = jnp.concatenate(
            [jnp.sum(k_ref[0, j * t:(j + 1) * t, p * PAIR:(p + 1) * PAIR].astype(F32), axis=0, keepdims=True)
             for j in range(n_blk)], axis=0) * (1.0 / t)
        hi = kmean.astype(BF16)
        km_hi.append(hi)
        km_lo.append((kmean - hi.astype(F32)).astype(BF16))
        for j in range(n_blk):
            vt_scr[p, j, 0:PAIR, :] = v_ref[0, j * t:(j + 1) * t, p * PAIR:(p + 1) * PAIR].astype(F32).T.astype(BF16)
            vt_scr[p, j, PAIR:, :] = jnp.ones((MOBA_SUM_ROWS, t), BF16)

    def q_block(i, _):
        q_rows = _rows(i * t, t, t)
        heads = range(n_heads)
        qms = [jnp.where(fmasks[h % 2], q_ref[0, q_rows, lanes(h)].astype(F32).T, 0.0).astype(BF16)
               for h in heads]
        qts = [qm * SCALE for qm in qms]
        gates = [_dot(km_hi[h // 2], qms[h]) + _dot(km_lo[h // 2], qms[h]) for h in heads]
        for h in heads:
            gate = jnp.where(blk_id < i, gates[h], NEG)
            rank = jnp.zeros((n_blk, t), jnp.int32)
            for mth in range(n_blk):
                gm = gate[mth:mth + 1, :]
                ahead = (gm > gate) | ((gm == gate) & (mth < blk_id))
                rank = rank + ahead.astype(jnp.int32)
            chosen = ((rank < MOBA_TOPK) & (blk_id < i)) | (blk_id == i)
            sel_scr[h] = jnp.where(chosen, 0.0, NEG)
        state = []
        for h in heads:
            state += [jnp.full((1, t), NEG, F32), jnp.zeros((PAIR + MOBA_SUM_ROWS, t), F32)]

        def k_blocks(js, state):
            k_rows = [_rows(j * t, t, t) for j in js]
            ss = [[_dot(k_ref[0, kr, lanes(h)], qts[h]) for kr in k_rows] for h in heads]
            ms, alphas, ps = [], [], []
            for h in heads:
                m = state[2 * h]
                s = [ss[h][b] + bias_ref[i - j, h] + sel_scr[h, pl.ds(j, 1), :] for b, j in enumerate(js)]
                m_new = m
                for sb in s:
                    m_new = jnp.maximum(m_new, jnp.max(sb, axis=0, keepdims=True))
                ms.append(m_new)
                alphas.append(jnp.exp(m - m_new))
                ps.append(jnp.concatenate([jnp.exp((sb - m_new).astype(BF16)) for sb in s], axis=0))
            pvs = [_dot(jnp.concatenate([vt_scr[h // 2, j] for j in js], axis=1), ps[h]) for h in heads]
            new_state = []
            for h in heads:
                new_state += [ms[h], state[2 * h + 1] * alphas[h] + pvs[h]]
            return tuple(new_state)

        def block_at(k):
            return jnp.where(k == 0, i, k - 1)

        odd = (i + 1) & 1
        state = lax.fori_loop(0, odd, lambda _, st: k_blocks([i], st), tuple(state))
        state = lax.fori_loop(0, (i + 1) >> 1,
                              lambda g, st: k_blocks([block_at(odd + 2 * g), block_at(odd + 2 * g + 1)], st), state)
        outs = [state[2 * h + 1][:PAIR] / state[2 * h + 1][PAIR:PAIR + 1] for h in heads]
        for p in range(2):
            o_ref[0, q_rows, p * PAIR:(p + 1) * PAIR] = jnp.where(
                fmasks[0], outs[2 * p], outs[2 * p + 1]).T.astype(o_ref.dtype)
        return 0

    lax.fori_loop(0, n_blk, q_block, 0)


def _mixer_d(proj, bias_d):
    b, seq, _ = proj.shape
    blk = COL_D // GROUP_W
    n_blk = seq // MOBA_BLOCK

    def col(j):
        return pl.BlockSpec((1, seq, GROUP_W), lambda i: (i, 0, blk + j))

    return pl.pallas_call(
        _mixer_d_kernel,
        grid=(b,),
        in_specs=[col(0), col(1), col(2),
                  pl.BlockSpec(bias_d.shape, lambda i: (0, 0, 0, 0))],
        out_specs=pl.BlockSpec((1, seq, GROUP_W), lambda i: (i, 0, 0)),
        out_shape=jax.ShapeDtypeStruct((b, seq, GROUP_W), BF16),
        scratch_shapes=[pltpu.VMEM((2, n_blk, PAIR + MOBA_SUM_ROWS, MOBA_BLOCK), BF16),
                        pltpu.VMEM((4, n_blk, MOBA_BLOCK), F32)],
        compiler_params=_cparams("parallel"),
        name="mixer_d_moba",
    )(proj, proj, proj, bias_d)


def _mix_out_rows(x, ys, g_ref, w_ref):
    acc = x
    for gi, y in enumerate(ys):
        cols = slice(gi * GROUP_W, (gi + 1) * GROUP_W)
        yn = _rms(y.astype(F32), g_ref[:, cols]).astype(BF16)
        acc = acc + _dot(yn, w_ref[cols, :])
    return acc


def _cross_rows(xs, g_ref, wq_ref, kv_ref, wo_ref):
    tm = xs[0].shape[0]
    heads = range(4)
    work = [(c, h) for c in range(len(xs)) for h in heads]

    def lanes(h, lo=0):
        return slice(lo + (h // 2) * PAIR, lo + (h // 2 + 1) * PAIR)

    hs = [_rms(x, g_ref[...]).astype(BF16) for x in xs]
    qs = [(_dot(h, wq_ref[...]) * SCALE).astype(BF16) for h in hs]
    qms = [jnp.where(_head_mask((tm, PAIR), h % 2), qs[c][:, lanes(h)], 0.0).astype(BF16) for c, h in work]
    ss = [_dot_nt(qms[w], kv_ref[:, lanes(h)]) for w, (c, h) in enumerate(work)]
    ls, ps = [], []
    for s in ss:
        m = jnp.max(s, axis=-1, keepdims=True)
        p = jnp.exp(s - m)
        ls.append(jnp.sum(p, axis=-1, keepdims=True))
        ps.append(p.astype(BF16))
    pvs = [_dot(ps[w], kv_ref[:, lanes(h, X_W)]) for w, (c, h) in enumerate(work)]
    outs = []
    for c, x in enumerate(xs):
        os = [jnp.where(_head_mask((tm, PAIR), 0), pvs[4 * c + 2 * p] / ls[4 * c + 2 * p],
                        pvs[4 * c + 2 * p + 1] / ls[4 * c + 2 * p + 1]).astype(BF16) for p in range(2)]
        outs.append(x + _dot(os[0], wo_ref[0:PAIR, :]) + _dot(os[1], wo_ref[PAIR:2 * PAIR, :]))
    return outs


POST_CHUNK = 256


def _post_kernel(x_ref, ya_ref, yb_ref, yc_ref, yd_ref, gg_ref, wout_ref, gc_ref, wq_ref, kv_ref, wo_ref,
                 gm_ref, wu_ref, wd_ref, gf_ref, o_ref, xn_scr, acc_scr, *, final_norm):
    j = pl.program_id(1)

    @pl.when(j == 0)
    def _():
        chunks = [pl.ds(c * POST_CHUNK, POST_CHUNK) for c in range(x_ref.shape[0] // POST_CHUNK)]
        for rows in chunks:
            ys = [y_ref[rows, :] for y_ref in (ya_ref, yb_ref, yc_ref, yd_ref)]
            acc_scr[rows, :] = _mix_out_rows(x_ref[rows, :], ys, gg_ref, wout_ref)
        xs = _cross_rows([acc_scr[rows, :] for rows in chunks], gc_ref, wq_ref, kv_ref, wo_ref)
        for rows, x in zip(chunks, xs):
            xn_scr[rows, :] = _rms(x, gm_ref[...]).astype(BF16)
            acc_scr[rows, :] = x

    h = _dot(xn_scr[...], wu_ref[...])
    h = jnp.square(jnp.maximum(h, 0.0)).astype(BF16)
    acc_scr[...] += _dot(h, wd_ref[...])

    @pl.when(j == pl.num_programs(1) - 1)
    def _():
        y = acc_scr[...]
        if final_norm:
            y = _rms(y, gf_ref[...])
        o_ref[...] = y


def _post(x, ys, g_group, w_out, g_cross, wq, kv, wo, g_mlp, wu, wd, g_final, final_norm, seq, tm, tf):
    t, d = x.shape
    ff = wu.shape[1]
    mem_len = kv.shape[1]
    tiles_per_seq = seq // tm

    def const(shape):
        return pl.BlockSpec(shape, lambda i, j: (0,) * len(shape))

    y_spec = pl.BlockSpec((tm, GROUP_W), lambda i, j: (i, 0))
    return pl.pallas_call(
        functools.partial(_post_kernel, final_norm=final_norm),
        grid=(t // tm, ff // tf),
        in_specs=[pl.BlockSpec((tm, d), lambda i, j: (i, 0)), y_spec, y_spec, y_spec, y_spec,
                  const((1, 4 * GROUP_W)), const((4 * GROUP_W, d)),
                  const((1, d)), const((d, X_W)),
                  pl.BlockSpec((None, mem_len, 2 * X_W), lambda i, j: (i // tiles_per_seq, 0, 0)),
                  const((X_W, d)),
                  const((1, d)),
                  pl.BlockSpec((d, tf), lambda i, j: (0, j)),
                  pl.BlockSpec((tf, d), lambda i, j: (j, 0)),
                  const((1, d))],
        out_specs=pl.BlockSpec((tm, d), lambda i, j: (i, 0)),
        out_shape=jax.ShapeDtypeStruct((t, d), F32),
        scratch_shapes=[pltpu.VMEM((tm, d), BF16), pltpu.VMEM((tm, d), F32)],
        compiler_params=pltpu.CompilerParams(dimension_semantics=("parallel", "arbitrary"),
                                             vmem_limit_bytes=POST_VMEM_LIMIT),
        name="post_mixer_block",
    )(x, *ys, g_group.reshape(1, -1), w_out, g_cross.reshape(1, d), wq, kv, wo,
      g_mlp.reshape(1, d), wu, wd, g_final.reshape(1, d))


def _permute_c_heads(a, axis, lo):
    parts = [lax.slice_in_dim(a, 0, lo, axis=axis)]
    parts += [lax.slice_in_dim(a, lo + h * HEAD_DIM, lo + (h + 1) * HEAD_DIM, axis=axis) for h in C_HEAD_ORDER]
    parts.append(lax.slice_in_dim(a, lo + GROUP_W, a.shape[axis], axis=axis))
    return jnp.concatenate(parts, axis=axis)


def kernel(x, mem, rel_table, g_mix, w_in, g_group, sinks, w_out, g_cross, g_mem, w_xq, w_xkv, w_xo,
           g_mlp, w_up, w_down, g_final):
    b, seq, d = x.shape
    mem_len = mem.shape[1]
    depth = w_in.shape[0]
    tokens = b * seq

    bkt_a, bkt_c, bkt_d = _static_buckets()
    tab_t = rel_table.astype(F32).T
    bias_a = _rel_bias(tab_t, jnp.asarray(bkt_a), A_BIAS_LO)
    bias_c = _rel_bias(tab_t, jnp.asarray(bkt_c), C_BIAS_LO)
    bias_d = _rel_bias(tab_t, jnp.asarray(bkt_d), D_BIAS_LO)
    tri = np.tril(np.ones((SB_BLOCK, SB_BLOCK), np.float32))
    tri = jnp.asarray(np.concatenate([tri, tri], axis=0), dtype=BF16)

    c_out_lo = 2 * GROUP_W
    xf = x.reshape(tokens, d)
    mem_f = mem.reshape(b * mem_len, d)
    for l in range(depth):
        w_in_l = _permute_c_heads(w_in[l], 1, COL_CQ).astype(BF16)
        g_group_l = _permute_c_heads(g_group[l], 0, c_out_lo)
        w_out_l = _permute_c_heads(w_out[l], 0, c_out_lo).astype(BF16)

        proj = _norm_matmul(xf, g_mix[l], w_in_l, tm=1024).reshape(b, seq, IN_WIDTH)
        ya = _mixer_a(proj, bias_a)
        yb = _mixer_b(proj, tri)
        yc = _mixer_c(proj, bias_c, sinks[l].astype(F32))
        yd = _mixer_d(proj, bias_d)
        ys = [y.reshape(tokens, GROUP_W) for y in (ya, yb, yc, yd)]
        kv = _norm_matmul(mem_f, g_mem[l], w_xkv[l].astype(BF16), tm=512).reshape(b, mem_len, 2 * X_W)
        xf = _post(xf, ys, g_group_l, w_out_l, g_cross[l], w_xq[l].astype(BF16), kv, w_xo[l].astype(BF16),
                   g_mlp[l], w_up[l].astype(BF16), w_down[l].astype(BF16), g_final,
                   final_norm=(l == depth - 1), seq=seq, tm=1024, tf=1024)
    return xf.reshape(b, seq, d)
```
